```python
import math
import jax, jax.numpy as jnp
from jax import lax
import numpy as np

D_MODEL = 2048
BATCH = 8
SEQ = 2048
DEPTH = 2

N_EVEN = (DEPTH + 1) // 2
N_ODD = DEPTH // 2

D_FF = 256 * ((8 * D_MODEL + 3 * 256 - 1) // (3 * 256))

A_WIDTH = D_MODEL
A_CHUNK = 128
A_GROUP = 128
A_HEADS = A_WIDTH // A_GROUP

B_HEAD_DIM = 64
B_WIDTH = D_MODEL
B_HEADS = B_WIDTH // B_HEAD_DIM
B_GROUPS = 4
B_HPG = B_HEADS // B_GROUPS
B_STATE = 128
B_CONV = 4
B_CHUNK = 128
B_CONV_CH = B_WIDTH + 2 * B_GROUPS * B_STATE

AB_IN = 2 * A_WIDTH + B_WIDTH + B_CONV_CH + B_HEADS
AB_MIX = A_WIDTH + B_WIDTH

ATT_HEAD_DIM = 128
N_ATT_HEADS = D_MODEL // ATT_HEAD_DIM
D_HEADS = N_ATT_HEADS // 4
C_HEADS = N_ATT_HEADS - D_HEADS
C_WIDTH = C_HEADS * ATT_HEAD_DIM
D_WIDTH = D_HEADS * ATT_HEAD_DIM
CD_IN = 3 * (C_WIDTH + D_WIDTH)
CD_MIX = C_WIDTH + D_WIDTH
C_QBLOCK = 128
MOBA_BLOCK = 256
MOBA_TOPK = 3
MOBA_QSUB = 32

kernel_name = "hybrid_gmlp_ssd_stickbreak_moba_block"


def rms_norm(x, g, eps=1e-6):
    xf = x.astype(jnp.float32)
    y = xf * lax.rsqrt(jnp.mean(xf * xf, axis=-1, keepdims=True) + eps)
    return (y * g.astype(jnp.float32)).astype(x.dtype)


def layer_norm(x, g, b, eps=1e-5):
    xf = x.astype(jnp.float32)
    mu = jnp.mean(xf, axis=-1, keepdims=True)
    var = jnp.mean(jnp.square(xf - mu), axis=-1, keepdims=True)
    y = (xf - mu) * lax.rsqrt(var + eps) * g.astype(jnp.float32) + b.astype(jnp.float32)
    return y.astype(x.dtype)


def swiglu(h, w_gate, w_up, w_down):
    a = jax.nn.silu(jnp.einsum('bsd,df->bsf', h, w_gate)) * jnp.einsum('bsd,df->bsf', h, w_up)
    return jnp.einsum('bsf,fd->bsd', a, w_down)


def chunked_spatial_gating(u, v, ln_g, ln_b, w_s, b_s):
    bsz, s, _ = v.shape
    n_chunks = s // A_CHUNK
    v = layer_norm(v, ln_g, ln_b).reshape(bsz, n_chunks, A_CHUNK, A_HEADS, A_GROUP)
    causal = jnp.tril(jnp.ones((A_CHUNK, A_CHUNK), dtype=bool))
    w = jnp.where(causal, w_s, 0.0).astype(v.dtype)
    mixed = jnp.einsum('hts,bcshe->bcthe', w, v) + b_s.T.astype(v.dtype)[:, :, None]
    return u * mixed.reshape(bsz, s, A_WIDTH)


def causal_depthwise_conv(x, w, b):
    k_w, ch = w.shape
    y = lax.conv_general_dilated(
        x, w[:, None, :].astype(x.dtype), window_strides=(1,), padding=[(k_w - 1, 0)],
        dimension_numbers=('NWC', 'WIO', 'NWC'), feature_group_count=ch)
    return y + b.astype(x.dtype)


def ssd_scan(x, dt, a, b_mat, c_mat):
    bsz, s, g, r, p = x.shape
    n = b_mat.shape[-1]
    nc, L = s // B_CHUNK, B_CHUNK
    xc = (x * dt[..., None]).reshape(bsz, nc, L, g, r, p)
    a_dt = (dt * a).reshape(bsz, nc, L, g, r).transpose(0, 1, 3, 4, 2)
    bc = b_mat.reshape(bsz, nc, L, g, n)
    cc = c_mat.reshape(bsz, nc, L, g, n)
    a_cum = jnp.cumsum(a_dt, axis=-1)
    causal = jnp.tril(jnp.ones((L, L), dtype=bool))
    seg = a_cum[..., :, None] - a_cum[..., None, :]
    decay = jnp.exp(jnp.where(causal, seg, -jnp.inf))
    cb = jnp.einsum('bctgn,bcsgn->bcgts', cc, bc)
    y_diag = jnp.einsum('bcgrts,bcsgrp->bctgrp', cb[:, :, :, None] * decay, xc)
    decay_to_end = jnp.exp(a_cum[..., -1:] - a_cum)
    states = jnp.einsum('bcsgn,bcgrs,bcsgrp->bcgrpn', bc, decay_to_end, xc)
    chunk_decay = jnp.exp(a_cum[..., -1])

    def step(h, inp):
        s_c, d_c = inp
        return h * d_c[..., None, None] + s_c, h

    h0 = jnp.zeros((bsz, g, r, p, n), jnp.float32)
    _, prev = lax.scan(step, h0, (jnp.moveaxis(states, 1, 0), jnp.moveaxis(chunk_decay, 1, 0)))
    prev = jnp.moveaxis(prev, 0, 1)
    y_off = jnp.einsum('bctgn,bcgrpn,bcgrt->bctgrp', cc, prev, jnp.exp(a_cum))
    return (y_diag + y_off).reshape(bsz, s, g, r, p)


def ssd_mixer(z, xbc, dt_raw, conv_w, conv_b, dt_bias, a_log, d_skip, norm_g):
    bsz, s, _ = z.shape
    f32 = jnp.float32
    xbc = jax.nn.silu(causal_depthwise_conv(xbc, conv_w, conv_b))
    xs, bm, cm = jnp.split(xbc, [B_WIDTH, B_WIDTH + B_GROUPS * B_STATE], axis=-1)
    xs = xs.astype(f32).reshape(bsz, s, B_GROUPS, B_HPG, B_HEAD_DIM)
    bm = bm.astype(f32).reshape(bsz, s, B_GROUPS, B_STATE)
    cm = cm.astype(f32).reshape(bsz, s, B_GROUPS, B_STATE)
    dt = jax.nn.softplus(dt_raw.astype(f32) + dt_bias.astype(f32)).reshape(bsz, s, B_GROUPS, B_HPG)
    a = -jnp.exp(a_log.astype(f32)).reshape(B_GROUPS, B_HPG)
    y = ssd_scan(xs, dt, a, bm, cm) + d_skip.astype(f32).reshape(B_GROUPS, B_HPG)[..., None] * xs
    y = y.reshape(bsz, s, B_WIDTH).astype(z.dtype)
    return rms_norm(y * jax.nn.silu(z), norm_g)


def stick_breaking_attention(q, k, v):
    bsz, s, h, dh = q.shape
    scale = dh ** -0.5
    outs = []
    for i in range(s // C_QBLOCK):
        q0 = i * C_QBLOCK
        kv_len = q0 + C_QBLOCK
        logits = jnp.einsum('bqhd,bkhd->bhqk', q[:, q0:kv_len], k[:, :kv_len]).astype(jnp.float32) * scale
        q_pos = q0 + jnp.arange(C_QBLOCK)
        k_pos = jnp.arange(kv_len)
        past = k_pos[None, :] < q_pos[:, None]
        log_beta = jax.nn.log_sigmoid(logits)
        log_keep = jnp.where(past, jax.nn.log_sigmoid(-logits), 0.0)
        log_w = log_beta + lax.cumsum(log_keep, axis=3, reverse=True) - log_keep
        w = jnp.where(past, jnp.exp(log_w), 0.0).astype(v.dtype)
        outs.append(jnp.einsum('bhqk,bkhd->bqhd', w, v[:, :kv_len]))
    return jnp.concatenate(outs, axis=1)


def moba_attention(q, k, v):
    bsz, s, h, dh = q.shape
    f32 = jnp.float32
    scale = dh ** -0.5
    n_blk = -(-s // MOBA_BLOCK)
    s_pad = n_blk * MOBA_BLOCK
    pad = ((0, 0), (0, s_pad - s), (0, 0), (0, 0))
    q, k, v = jnp.pad(q, pad), jnp.pad(k, pad), jnp.pad(v, pad)
    kb = k.reshape(bsz, n_blk, MOBA_BLOCK, h, dh)
    vb = v.reshape(bsz, n_blk, MOBA_BLOCK, h, dh)
    k_mean = jnp.mean(kb.astype(f32), axis=2)
    gate = jnp.einsum('bshd,bnhd->bshn', q.astype(f32), k_mean)
    q_blk = jnp.arange(s_pad) // MOBA_BLOCK
    fully_past = jnp.arange(n_blk)[None, :] < q_blk[:, None]
    gate = jnp.where(fully_past[None, :, None, :], gate, -jnp.inf)
    n_sel = min(MOBA_TOPK, n_blk)
    _, sel = lax.top_k(gate, n_sel)
    sel_ok = sel < q_blk[None, :, None, None]
    kbh = kb.transpose(0, 3, 1, 2, 4)
    vbh = vb.transpose(0, 3, 1, 2, 4)
    b_ix = jnp.arange(bsz)[:, None, None, None]
    h_ix = jnp.arange(h)[None, None, :, None]

    def sub_block(n):
        start = n * MOBA_QSUB
        q_n = lax.dynamic_slice_in_dim(q, start, MOBA_QSUB, axis=1)
        sel_n = lax.dynamic_slice_in_dim(sel, start, MOBA_QSUB, axis=1)
        ok_n = lax.dynamic_slice_in_dim(sel_ok, start, MOBA_QSUB, axis=1)
        own_start = (start // MOBA_BLOCK) * MOBA_BLOCK
        k_own = lax.dynamic_slice_in_dim(k, own_start, MOBA_BLOCK, axis=1)
        v_own = lax.dynamic_slice_in_dim(v, own_start, MOBA_BLOCK, axis=1)
        k_sel = kbh[b_ix, h_ix, sel_n]
        v_sel = vbh[b_ix, h_ix, sel_n]
        s_sel = jnp.einsum('bqhd,bqhkcd->bqhkc', q_n, k_sel).astype(f32) * scale
        s_sel = jnp.where(ok_n[..., None], s_sel, -jnp.inf).reshape(bsz, MOBA_QSUB, h, n_sel * MOBA_BLOCK)
        q_pos = start + jnp.arange(MOBA_QSUB)
        k_pos = own_start + jnp.arange(MOBA_BLOCK)
        causal = k_pos[None, :] <= q_pos[:, None]
        s_own = jnp.einsum('bqhd,bchd->bqhc', q_n, k_own).astype(f32) * scale
        s_own = jnp.where(causal[None, :, None, :], s_own, -jnp.inf)
        p = jax.nn.softmax(jnp.concatenate([s_sel, s_own], axis=-1), axis=-1).astype(v.dtype)
        p_sel = p[..., :n_sel * MOBA_BLOCK].reshape(bsz, MOBA_QSUB, h, n_sel, MOBA_BLOCK)
        p_own = p[..., n_sel * MOBA_BLOCK:]
        return (jnp.einsum('bqhkc,bqhkcd->bqhd', p_sel, v_sel)
                + jnp.einsum('bqhc,bchd->bqhd', p_own, v_own))

    out = lax.map(sub_block, jnp.arange(s_pad // MOBA_QSUB))
    out = jnp.moveaxis(out, 0, 1).reshape(bsz, s_pad, h, dh)
    return out[:, :s]


def ab_mixer(hm, w_in, w_out, gm_ln_g, gm_ln_b, gm_w_s, gm_b_s,
             conv_w, conv_b, dt_bias, a_log, d_skip, ssd_norm_g):
    proj = jnp.einsum('bsd,de->bse', hm, w_in)
    u, v, z, xbc, dt_raw = jnp.split(
        proj, [A_WIDTH, 2 * A_WIDTH, 2 * A_WIDTH + B_WIDTH, 2 * A_WIDTH + B_WIDTH + B_CONV_CH], axis=-1)
    y_a = chunked_spatial_gating(jax.nn.gelu(u), jax.nn.gelu(v), gm_ln_g, gm_ln_b, gm_w_s, gm_b_s)
    y_b = ssd_mixer(z, xbc, dt_raw, conv_w, conv_b, dt_bias, a_log, d_skip, ssd_norm_g)
    return jnp.einsum('bse,ed->bsd', jnp.concatenate([y_a, y_b], axis=-1), w_out)


def cd_mixer(hm, w_in, w_out, q_norm_g, k_norm_g):
    bsz, s, _ = hm.shape
    proj = jnp.einsum('bsd,de->bse', hm, w_in)
    qc, kc, vc, qd, kd, vd = jnp.split(
        proj, [C_WIDTH, 2 * C_WIDTH, 3 * C_WIDTH, 3 * C_WIDTH + D_WIDTH, 3 * C_WIDTH + 2 * D_WIDTH], axis=-1)
    heads_c = lambda t: t.reshape(bsz, s, C_HEADS, ATT_HEAD_DIM)
    heads_d = lambda t: t.reshape(bsz, s, D_HEADS, ATT_HEAD_DIM)
    y_c = stick_breaking_attention(heads_c(qc), heads_c(kc), heads_c(vc))
    y_d = moba_attention(rms_norm(heads_d(qd), q_norm_g), rms_norm(heads_d(kd), k_norm_g), heads_d(vd))
    y = jnp.concatenate([y_c.reshape(bsz, s, C_WIDTH), y_d.reshape(bsz, s, D_WIDTH)], axis=-1)
    return jnp.einsum('bse,ed->bsd', y, w_out)


def setup_inputs(seed: int = 0) -> dict:
    key = jax.random.key(seed)
    ks = iter(jax.random.split(key, 40))
    f32 = jnp.float32

    def nrm(shape, scale):
        return jax.random.normal(next(ks), shape, f32) * scale

    E, O = N_EVEN, N_ODD
    x = nrm((BATCH, SEQ, D_MODEL), 1.0)
    c = nrm((BATCH, D_MODEL), 1.0)
    norm_mix_g = 1.0 + nrm((DEPTH, D_MODEL), 0.05)
    norm_ffn_g = 1.0 + nrm((DEPTH, D_MODEL), 0.05)
    ada_w = nrm((DEPTH, D_MODEL, 6 * D_MODEL), 0.5 * D_MODEL ** -0.5)
    ada_b = nrm((DEPTH, 6 * D_MODEL), 0.05)
    ffn_w_gate = nrm((DEPTH, D_MODEL, D_FF), D_MODEL ** -0.5)
    ffn_w_up = nrm((DEPTH, D_MODEL, D_FF), D_MODEL ** -0.5)
    ffn_w_down = nrm((DEPTH, D_FF, D_MODEL), D_FF ** -0.5)
    ab_w_in = nrm((E, D_MODEL, AB_IN), D_MODEL ** -0.5)
    ab_w_out = nrm((E, AB_MIX, D_MODEL), AB_MIX ** -0.5)
    gm_ln_g = 1.0 + nrm((E, A_WIDTH), 0.05)
    gm_ln_b = nrm((E, A_WIDTH), 0.02)
    gm_w_s = nrm((E, A_HEADS, A_CHUNK, A_CHUNK), A_CHUNK ** -0.5)
    gm_b_s = 1.0 + nrm((E, A_HEADS, A_CHUNK), 0.05)
    ssd_conv_w = nrm((E, B_CONV, B_CONV_CH), B_CONV ** -0.5)
    ssd_conv_b = nrm((E, B_CONV_CH), 0.02)
    dt0 = jnp.exp(jax.random.uniform(next(ks), (E, B_HEADS), f32,
                                     minval=math.log(1e-3), maxval=math.log(1e-1)))
    ssd_dt_bias = dt0 + jnp.log(-jnp.expm1(-dt0))
    ssd_a_log = jnp.log(jax.random.uniform(next(ks), (E, B_HEADS), f32, minval=1.0, maxval=16.0))
    ssd_d = 1.0 + nrm((E, B_HEADS), 0.1)
    ssd_norm_g = 1.0 + nrm((E, B_WIDTH), 0.05)
    cd_w_in = nrm((O, D_MODEL, CD_IN), D_MODEL ** -0.5)
    cd_w_out = nrm((O, CD_MIX, D_MODEL), CD_MIX ** -0.5)
    moba_q_norm_g = 1.0 + nrm((O, ATT_HEAD_DIM), 0.05)
    moba_k_norm_g = 1.0 + nrm((O, ATT_HEAD_DIM), 0.05)
    return {"x": x, "c": c, "norm_mix_g": norm_mix_g, "norm_ffn_g": norm_ffn_g,
            "ada_w": ada_w, "ada_b": ada_b, "ffn_w_gate": ffn_w_gate, "ffn_w_up": ffn_w_up,
            "ffn_w_down": ffn_w_down, "ab_w_in": ab_w_in, "ab_w_out": ab_w_out,
            "gm_ln_g": gm_ln_g, "gm_ln_b": gm_ln_b, "gm_w_s": gm_w_s, "gm_b_s": gm_b_s,
            "ssd_conv_w": ssd_conv_w, "ssd_conv_b": ssd_conv_b, "ssd_dt_bias": ssd_dt_bias,
            "ssd_a_log": ssd_a_log, "ssd_d": ssd_d, "ssd_norm_g": ssd_norm_g,
            "cd_w_in": cd_w_in, "cd_w_out": cd_w_out,
            "moba_q_norm_g": moba_q_norm_g, "moba_k_norm_g": moba_k_norm_g}


def reference(x, c, norm_mix_g, norm_ffn_g, ada_w, ada_b, ffn_w_gate, ffn_w_up, ffn_w_down,
              ab_w_in, ab_w_out, gm_ln_g, gm_ln_b, gm_w_s, gm_b_s,
              ssd_conv_w, ssd_conv_b, ssd_dt_bias, ssd_a_log, ssd_d, ssd_norm_g,
              cd_w_in, cd_w_out, moba_q_norm_g, moba_k_norm_g):
    c_act = jax.nn.silu(c)
    for layer in range(DEPTH):
        mod = (c_act @ ada_w[layer] + ada_b[layer])[:, None, :]
        sh_m, sc_m, g_m, sh_f, sc_f, g_f = jnp.split(mod, 6, axis=-1)
        hm = rms_norm(x, norm_mix_g[layer]) * (1.0 + sc_m) + sh_m
        if layer % 2 == 0:
            e = layer // 2
            y = ab_mixer(hm, ab_w_in[e], ab_w_out[e], gm_ln_g[e], gm_ln_b[e], gm_w_s[e], gm_b_s[e],
                         ssd_conv_w[e], ssd_conv_b[e], ssd_dt_bias[e], ssd_a_log[e], ssd_d[e],
                         ssd_norm_g[e])
        else:
            o = layer // 2
            y = cd_mixer(hm, cd_w_in[o], cd_w_out[o], moba_q_norm_g[o], moba_k_norm_g[o])
        x = x + (1.0 + g_m) * y
        hf = rms_norm(x, norm_ffn_g[layer]) * (1.0 + sc_f) + sh_f
        x = x + (1.0 + g_f) * swiglu(hf, ffn_w_gate[layer], ffn_w_up[layer], ffn_w_down[layer])
    return x
```

```python
import functools

import jax
import jax.numpy as jnp
from jax import lax
from jax.experimental import pallas as pl
from jax.experimental.pallas import tpu as pltpu

F32 = jnp.float32
BF16 = jnp.bfloat16

LANES = 128
A_CHUNK = 128
A_GROUP = 128
SSD_HEAD_DIM = 64
SSD_GROUPS = 4
SSD_STATE = 128
SSD_CONV = 4
SSD_CHUNK = 128
ATT_HEAD_DIM = 128
MOBA_BLOCK = 256
MOBA_TOPK = 3
SB_TILE = 128
MIB = 1 << 20


def _params(semantics, vmem_mib):
    return pltpu.CompilerParams(dimension_semantics=semantics, vmem_limit_bytes=vmem_mib * MIB)


def _split_bf16(v):
    hi = v.astype(BF16)
    lo = (v - hi.astype(F32)).astype(BF16)
    return hi, lo


def _split3_bf16(v):
    hi = v.astype(BF16)
    r = v - hi.astype(F32)
    mid = r.astype(BF16)
    lo = (r - mid.astype(F32)).astype(BF16)
    return hi, mid, lo


def _dot(a, b):
    return jnp.dot(a, b, preferred_element_type=F32)


def _dot_nt(a, b):
    return lax.dot_general(a, b, (((1,), (1,)), ((), ())), preferred_element_type=F32)


def _silu(v):
    return v * jax.nn.sigmoid(v)


def _ada_kernel(c_ref, w_ref, b_ref, o_ref):
    ca = _silu(c_ref[...])
    o_ref[0] = _dot(ca.astype(BF16), w_ref[0].astype(BF16)) + b_ref[0]


def _ada_mod(c, ada_w, ada_b):
    depth, d, n = ada_w.shape
    bsz = c.shape[0]
    tn = 1024
    return pl.pallas_call(
        _ada_kernel,
        out_shape=jax.ShapeDtypeStruct((depth, bsz, n), F32),
        grid=(depth, n // tn),
        in_specs=[
            pl.BlockSpec((bsz, d), lambda l, j: (0, 0)),
            pl.BlockSpec((1, d, tn), lambda l, j: (l, 0, j)),
            pl.BlockSpec((1, 1, tn), lambda l, j: (l, 0, j)),
        ],
        out_specs=pl.BlockSpec((1, bsz, tn), lambda l, j: (l, 0, j)),
        compiler_params=_params(("arbitrary", "arbitrary"), 40),
        name="ada_mod",
    )(c, ada_w, ada_b.reshape(depth, 1, n))


def _norm_mod(x, g, sc, sh):
    ms = jnp.mean(x * x, axis=-1, keepdims=True)
    y = x * lax.rsqrt(ms + 1e-6)
    return (y * g) * (1.0 + sc) + sh


def _inproj_kernel(*refs, n_act_tiles, has_extra):
    if has_extra:
        x_ref, g_ref, sc_ref, sh_ref, w_ref, wxh_ref, wxl_ref, o_ref, ox_ref, hm_ref = refs
    else:
        x_ref, g_ref, sc_ref, sh_ref, w_ref, o_ref, hm_ref = refs
    j = pl.program_id(2)

    @pl.when(j == 0)
    def _():
        h = _norm_mod(x_ref[0], g_ref[...], sc_ref[0], sh_ref[0])
        hi = h.astype(BF16)
        hm_ref[...] = hi
        if has_extra:
            lo = (h - hi.astype(F32)).astype(BF16)
            wxh = wxh_ref[...]
            ox_ref[0] = _dot(hi, wxh) + _dot(lo, wxh) + _dot(hi, wxl_ref[...])

    acc = _dot(hm_ref[...], w_ref[...])
    if n_act_tiles:
        @pl.when(j < n_act_tiles)
        def _():
            o_ref[0] = jax.nn.gelu(acc).astype(o_ref.dtype)

        @pl.when(j >= n_act_tiles)
        def _():
            o_ref[0] = acc.astype(o_ref.dtype)
    else:
        o_ref[0] = acc.astype(o_ref.dtype)


def _in_proj(x, g, sc, sh, w, *, n_act_cols=0, w_extra=None, tm=1024, tn=1024):
    bsz, s, d = x.shape
    n = w.shape[1]
    tm = min(tm, s)
    has_extra = w_extra is not None
    in_specs = [
        pl.BlockSpec((1, tm, d), lambda b, i, j: (b, i, 0)),
        pl.BlockSpec((1, d), lambda b, i, j: (0, 0)),
        pl.BlockSpec((1, 1, d), lambda b, i, j: (b, 0, 0)),
        pl.BlockSpec((1, 1, d), lambda b, i, j: (b, 0, 0)),
        pl.BlockSpec((d, tn), lambda b, i, j: (0, j)),
    ]
    args = [x, g.reshape(1, d), sc, sh, w]
    out_shape = [jax.ShapeDtypeStruct((bsz, s, n), BF16)]
    out_specs = [pl.BlockSpec((1, tm, tn), lambda b, i, j: (b, i, j))]
    if has_extra:
        wxh, wxl = _split_bf16(w_extra)
        in_specs += [pl.BlockSpec((d, LANES), lambda b, i, j: (0, 0))] * 2
        args += [wxh, wxl]
        out_shape.append(jax.ShapeDtypeStruct((bsz, s, LANES), F32))
        out_specs.append(pl.BlockSpec((1, tm, LANES), lambda b, i, j: (b, i, 0)))
    res = pl.pallas_call(
        functools.partial(_inproj_kernel, n_act_tiles=n_act_cols // tn, has_extra=has_extra),
        out_shape=out_shape,
        grid=(bsz, s // tm, n // tn),
        in_specs=in_specs,
        out_specs=out_specs,
        scratch_shapes=[pltpu.VMEM((tm, d), BF16)],
        compiler_params=_params(("parallel", "parallel", "arbitrary"), 48),
        name="in_proj",
    )(*args)
    return res if has_extra else res[0]


def _outproj_kernel(ya_ref, yb_ref, wa_ref, wb_ref, x_ref, gate_ref, o_ref):
    acc = _dot(ya_ref[0], wa_ref[...]) + _dot(yb_ref[0], wb_ref[...])
    o_ref[0] = x_ref[0] + (1.0 + gate_ref[0]) * acc


def _out_proj(ya, yb, wa, wb, x, gate, *, tm=1024, tn=512):
    bsz, s, d = x.shape
    ka, kb = ya.shape[2], yb.shape[2]
    tm = min(tm, s)
    return pl.pallas_call(
        _outproj_kernel,
        out_shape=jax.ShapeDtypeStruct((bsz, s, d), F32),
        grid=(bsz, s // tm, d // tn),
        in_specs=[
            pl.BlockSpec((1, tm, ka), lambda b, i, j: (b, i, 0)),
            pl.BlockSpec((1, tm, kb), lambda b, i, j: (b, i, 0)),
            pl.BlockSpec((ka, tn), lambda b, i, j: (0, j)),
            pl.BlockSpec((kb, tn), lambda b, i, j: (0, j)),
            pl.BlockSpec((1, tm, tn), lambda b, i, j: (b, i, j)),
            pl.BlockSpec((1, 1, tn), lambda b, i, j: (b, 0, j)),
        ],
        out_specs=pl.BlockSpec((1, tm, tn), lambda b, i, j: (b, i, j)),
        compiler_params=_params(("parallel", "parallel", "arbitrary"), 48),
        name="out_proj",
    )(ya, yb, wa, wb, x, gate)


def _ffn_kernel(x_ref, g_ref, sc_ref, sh_ref, gate_ref, wg_ref, wu_ref, wd_ref, o_ref, hf_ref, acc_ref):
    f = pl.program_id(2)

    @pl.when(f == 0)
    def _():
        hf_ref[...] = _norm_mod(x_ref[0], g_ref[...], sc_ref[0], sh_ref[0]).astype(BF16)

    hf = hf_ref[...]
    a = _silu(_dot(hf, wg_ref[...])) * _dot(hf, wu_ref[...])
    part = _dot(a.astype(BF16), wd_ref[...])

    @pl.when(f == 0)
    def _():
        acc_ref[...] = part

    @pl.when(f > 0)
    def _():
        acc_ref[...] += part

    @pl.when(f == pl.num_programs(2) - 1)
    def _():
        o_ref[0] = x_ref[0] + (1.0 + gate_ref[0]) * acc_ref[...]


def _ffn(x, g, sc, sh, gate, wg, wu, wd, *, tm=512, tf=512):
    bsz, s, d = x.shape
    dff = wg.shape[1]
    tm = min(tm, s)
    vec = pl.BlockSpec((1, 1, d), lambda b, i, f: (b, 0, 0))
    return pl.pallas_call(
        _ffn_kernel,
        out_shape=jax.ShapeDtypeStruct((bsz, s, d), F32),
        grid=(bsz, s // tm, dff // tf),
        in_specs=[
            pl.BlockSpec((1, tm, d), lambda b, i, f: (b, i, 0)),
            pl.BlockSpec((1, d), lambda b, i, f: (0, 0)),
            vec, vec, vec,
            pl.BlockSpec((d, tf), lambda b, i, f: (0, f)),
            pl.BlockSpec((d, tf), lambda b, i, f: (0, f)),
            pl.BlockSpec((tf, d), lambda b, i, f: (f, 0)),
        ],
        out_specs=pl.BlockSpec((1, tm, d), lambda b, i, f: (b, i, 0)),
        scratch_shapes=[pltpu.VMEM((tm, d), BF16), pltpu.VMEM((tm, d), F32)],
        compiler_params=_params(("parallel", "parallel", "arbitrary"), 48),
        name="ffn",
    )(x, g.reshape(1, d), sc, sh, gate, wg, wu, wd)


def _gmlp_kernel(u_ref, v_ref, lng_ref, lnb_ref, ws_ref, bst_ref, o_ref, vln_ref):
    ts, width = v_ref.shape[1], v_ref.shape[2]
    v = v_ref[0].astype(F32)
    mu = jnp.mean(v, axis=-1, keepdims=True)
    vc = v - mu
    var = jnp.mean(vc * vc, axis=-1, keepdims=True)
    vln_ref[...] = (vc * lax.rsqrt(var + 1e-5) * lng_ref[...] + lnb_ref[...]).astype(BF16)
    row = lax.broadcasted_iota(jnp.int32, (A_CHUNK, A_CHUNK), 0)
    col = lax.broadcasted_iota(jnp.int32, (A_CHUNK, A_CHUNK), 1)
    causal = col <= row
    for h in range(width // A_GROUP):
        w = jnp.where(causal, ws_ref[h], 0.0).astype(BF16)
        bias = bst_ref[:, h:h + 1]
        cols = slice(h * A_GROUP, (h + 1) * A_GROUP)
        for c in range(ts // A_CHUNK):
            rows = slice(c * A_CHUNK, (c + 1) * A_CHUNK)
            mixed = _dot(w, vln_ref[rows, cols]) + bias
            o_ref[0, rows, cols] = (u_ref[0, rows, cols].astype(F32) * mixed).astype(o_ref.dtype)


def _gmlp(proj, ln_g, ln_b, w_s, b_s, *, width, ts=512):
    bsz, s, _ = proj.shape
    heads = w_s.shape[0]
    ts = min(ts, s)
    return pl.pallas_call(
        _gmlp_kernel,
        out_shape=jax.ShapeDtypeStruct((bsz, s, width), BF16),
        grid=(bsz, s // ts),
        in_specs=[
            pl.BlockSpec((1, ts, width), lambda b, i: (b, i, 0)),
            pl.BlockSpec((1, ts, width), lambda b, i: (b, i, 1)),
            pl.BlockSpec((1, width), lambda b, i: (0, 0)),
            pl.BlockSpec((1, width), lambda b, i: (0, 0)),
            pl.BlockSpec((heads, A_CHUNK, A_CHUNK), lambda b, i: (0, 0, 0)),
            pl.BlockSpec((A_CHUNK, heads), lambda b, i: (0, 0)),
        ],
        out_specs=pl.BlockSpec((1, ts, width), lambda b, i: (b, i, 0)),
        scratch_shapes=[pltpu.VMEM((ts, width), BF16)],
        compiler_params=_params(("parallel", "parallel"), 40),
        name="gmlp",
    )(proj, proj, ln_g.reshape(1, width), ln_b.reshape(1, width), w_s, b_s.T)


def _pair_bcast(mat, q, low_half):
    rows = mat.shape[0]
    a = jnp.broadcast_to(mat[:, 2 * q:2 * q + 1], (rows, LANES))
    b = jnp.broadcast_to(mat[:, 2 * q + 1:2 * q + 2], (rows, LANES))
    return jnp.where(low_half, a, b)


def _ssd_kernel(z_ref, xbc_ref, dt_ref, cw_ref, cb_ref, dtb_ref, alog_ref, dskip_ref, ng_ref,
                o_ref, xpad_ref, xs_ref, state_ref, y_ref, *, width, n_bc):
    L = SSD_CHUNK
    c = pl.program_id(1)
    pad = 8

    @pl.when(c == 0)
    def _():
        xpad_ref[0:pad, :] = jnp.zeros((pad, xpad_ref.shape[1]), F32)
        state_ref[...] = jnp.zeros_like(state_ref)

    @pl.when(c > 0)
    def _():
        xpad_ref[0:pad, :] = xpad_ref[L:L + pad, :]

    xpad_ref[pad:pad + L, :] = xbc_ref[0].astype(F32)

    conv = cb_ref[...] + cw_ref[0:1, :] * xpad_ref[pad - 3:pad - 3 + L, :]
    for k in range(1, SSD_CONV):
        conv = conv + cw_ref[k:k + 1, :] * xpad_ref[pad - 3 + k:pad - 3 + k + L, :]
    xs_ref[...] = _silu(conv)

    dt = jax.nn.softplus(dt_ref[0] + dtb_ref[...])
    a_dt = dt * (-jnp.exp(alog_ref[...]))
    row = lax.broadcasted_iota(jnp.int32, (L, L), 0)
    col = lax.broadcasted_iota(jnp.int32, (L, L), 1)
    causal = col <= row
    tri = jnp.where(causal, 1.0, 0.0).astype(BF16)
    p0, p1, p2 = _split3_bf16(a_dt)
    a_cum = _dot(tri, p0) + _dot(tri, p1) + _dot(tri, p2)
    a_cum_t = a_cum.T
    a_last = a_cum[L - 1:L, :]
    e_cum = jnp.exp(a_cum)
    e_end = jnp.exp(a_last - a_cum)

    low_half = lax.broadcasted_iota(jnp.int32, (L, LANES), 1) < SSD_HEAD_DIM
    heads_per_group = width // SSD_HEAD_DIM // SSD_GROUPS
    pairs_per_group = heads_per_group // 2
    gw = width // SSD_GROUPS
    for g in range(SSD_GROUPS):
        bm = xs_ref[:, width + g * SSD_STATE:width + (g + 1) * SSD_STATE]
        cm = xs_ref[:, width + n_bc + g * SSD_STATE:width + n_bc + (g + 1) * SSD_STATE]
        bm16 = bm.astype(BF16)
        cm16 = cm.astype(BF16)
        cb = _dot_nt(cm16, bm16)
        y_off_g = _dot(cm16, state_ref[:, g * gw:(g + 1) * gw].astype(BF16))
        bm_t16 = bm.T.astype(BF16)
        for pq in range(pairs_per_group):
            q = g * pairs_per_group + pq
            lanes = slice(q * LANES, (q + 1) * LANES)
            xs_p = xs_ref[:, lanes]
            xc = xs_p * _pair_bcast(dt, q, low_half)
            ms = []
            for r in (2 * q, 2 * q + 1):
                seg = a_cum[:, r:r + 1] - a_cum_t[r:r + 1, :]
                decay = jnp.exp(jnp.where(causal, seg, -jnp.inf))
                ms.append((cb * decay).astype(BF16))
            xc16 = xc.astype(BF16)
            zero = jnp.zeros_like(xc16)
            rhs = jnp.concatenate([jnp.where(low_half, xc16, zero), jnp.where(low_half, zero, xc16)], axis=0)
            y_diag = _dot(jnp.concatenate(ms, axis=1), rhs)
            e_p = _pair_bcast(e_cum, q, low_half)
            y_p = y_diag + y_off_g[:, pq * LANES:(pq + 1) * LANES] * e_p + dskip_ref[:, lanes] * xs_p
            y_ref[:, lanes] = y_p
            xdec16 = (xc * _pair_bcast(e_end, q, low_half)).astype(BF16)
            s_new = _dot(bm_t16, xdec16)
            state_ref[:, lanes] = state_ref[:, lanes] * e_p[L - 1:L, :] + s_new

    z = z_ref[0].astype(F32)
    yz = y_ref[...] * _silu(z)
    ms = jnp.mean(yz * yz, axis=-1, keepdims=True)
    o_ref[0] = (yz * lax.rsqrt(ms + 1e-6) * ng_ref[...]).astype(o_ref.dtype)


def _ssd(proj, dt_raw, conv_w, conv_b, dt_bias, a_log, d_skip, norm_g, *, width, z_col, xbc_col):
    bsz, s, _ = proj.shape
    L = SSD_CHUNK
    conv_ch = conv_w.shape[1]
    n_bc = SSD_GROUPS * SSD_STATE
    heads = width // SSD_HEAD_DIM
    assert conv_ch == width + 2 * n_bc and z_col % width == 0 and xbc_col % conv_ch == 0

    def padl(v):
        return jnp.pad(v.astype(F32), (0, LANES - heads)).reshape(1, LANES)

    const = lambda shape: pl.BlockSpec(shape, lambda b, c: (0,) * len(shape))
    return pl.pallas_call(
        functools.partial(_ssd_kernel, width=width, n_bc=n_bc),
        out_shape=jax.ShapeDtypeStruct((bsz, s, width), BF16),
        grid=(bsz, s // L),
        in_specs=[
            pl.BlockSpec((1, L, width), lambda b, c: (b, c, z_col // width)),
            pl.BlockSpec((1, L, conv_ch), lambda b, c: (b, c, xbc_col // conv_ch)),
            pl.BlockSpec((1, L, LANES), lambda b, c: (b, c, 0)),
            const((SSD_CONV, conv_ch)),
            const((1, conv_ch)),
            const((1, LANES)),
            const((1, LANES)),
            const((1, width)),
            const((1, width)),
        ],
        out_specs=pl.BlockSpec((1, L, width), lambda b, c: (b, c, 0)),
        scratch_shapes=[
            pltpu.VMEM((L + 8, conv_ch), F32),
            pltpu.VMEM((L, conv_ch), F32),
            pltpu.VMEM((SSD_STATE, width), F32),
            pltpu.VMEM((L, width), F32),
        ],
        compiler_params=_params(("parallel", "arbitrary"), 40),
        name="ssd",
    )(proj, proj, dt_raw, conv_w, conv_b.reshape(1, conv_ch), padl(dt_bias), padl(a_log),
      jnp.repeat(d_skip.astype(F32), SSD_HEAD_DIM).reshape(1, width), norm_g.reshape(1, width))


def _log_sigmoid(v):
    return jnp.minimum(v, 0.0) - jnp.log1p(jnp.exp(-jnp.abs(v)))


def _sb_kernel(q_ref, k_ref, v_ref, o_ref, *, scale):
    T = SB_TILE
    i = pl.program_id(2)
    q = q_ref[0]
    row = lax.broadcasted_iota(jnp.int32, (T, T), 0)
    col = lax.broadcasted_iota(jnp.int32, (T, T), 1)
    suffix = jnp.where(row >= col, 1.0, 0.0).astype(BF16)
    suffix2 = jnp.concatenate([suffix, suffix], axis=0)

    def block(j, carry, acc, diagonal):
        start = pl.multiple_of(j * T, T)
        kj = k_ref[0, pl.ds(start, T), :]
        vj = v_ref[0, pl.ds(start, T), :]
        logits = _dot_nt(q, kj) * scale
        log_keep = _log_sigmoid(logits) - logits
        if diagonal:
            past = col < row
            log_keep = jnp.where(past, log_keep, 0.0)
        hi, lo = _split_bf16(log_keep)
        rinc = _dot(jnp.concatenate([hi, lo], axis=1), suffix2)
        w = jnp.exp(logits + rinc + carry)
        if diagonal:
            w = jnp.where(past, w, 0.0)
        acc = acc + _dot(w.astype(BF16), vj)
        return carry + rinc[:, 0:1], acc

    carry, acc = block(i, jnp.zeros((T, 1), F32), jnp.zeros((T, q.shape[1]), F32), True)

    def body(n, ca):
        return block(i - 1 - n, ca[0], ca[1], False)

    carry, acc = lax.fori_loop(0, i, body, (carry, acc))
    o_ref[0] = acc.astype(o_ref.dtype)


def _stick_breaking(proj, *, heads, q_col, k_col, v_col):
    bsz, s, _ = proj.shape
    dh = ATT_HEAD_DIM
    T = SB_TILE
    return pl.pallas_call(
        functools.partial(_sb_kernel, scale=dh ** -0.5),
        out_shape=jax.ShapeDtypeStruct((bsz, s, heads * dh), BF16),
        grid=(bsz, heads, s // T),
        in_specs=[
            pl.BlockSpec((1, T, dh), lambda b, h, i: (b, i, q_col // dh + h)),
            pl.BlockSpec((1, s, dh), lambda b, h, i: (b, 0, k_col // dh + h)),
            pl.BlockSpec((1, s, dh), lambda b, h, i: (b, 0, v_col // dh + h)),
        ],
        out_specs=pl.BlockSpec((1, T, dh), lambda b, h, i: (b, i, h)),
        compiler_params=_params(("parallel", "parallel", "arbitrary"), 32),
        name="stick_breaking",
    )(proj, proj, proj)


def _head_rms(v, g):
    ms = jnp.mean(v * v, axis=-1, keepdims=True)
    return v * lax.rsqrt(ms + 1e-6) * g


def _moba_kernel(q_ref, k_ref, v_ref, qg_ref, kg_ref, o_ref, kn_ref, kmh_ref, kml_ref, *, scale, n_blk):
    BLK = MOBA_BLOCK
    i = pl.program_id(2)

    @pl.when(i == 0)
    def _():
        kmh_ref[...] = jnp.zeros_like(kmh_ref)
        kml_ref[...] = jnp.zeros_like(kml_ref)
        for n in range(n_blk):
            rows = slice(n * BLK, (n + 1) * BLK)
            kn = _head_rms(k_ref[0, rows, :].astype(F32), kg_ref[...])
            kn_ref[rows, :] = kn.astype(BF16)
            hi, lo = _split_bf16(jnp.mean(kn, axis=0, keepdims=True))
            kmh_ref[n:n + 1, :] = hi
            kml_ref[n:n + 1, :] = lo

    qn = _head_rms(q_ref[0].astype(F32), qg_ref[...])
    qh, ql = _split_bf16(qn)
    kmh = kmh_ref[...]
    gate = _dot_nt(qh, kmh) + _dot_nt(ql, kmh) + _dot_nt(qh, kml_ref[...])
    lane = lax.broadcasted_iota(jnp.int32, (BLK, LANES), 1)
    gate = jnp.where(lane < i, gate, -jnp.inf)

    def selected(n):
        gn = gate[:, n:n + 1]
        ahead = (gate > gn) | ((gate == gn) & (lane < n))
        rank = jnp.sum(jnp.where(ahead & (lane < n_blk), 1.0, 0.0), axis=-1, keepdims=True)
        return rank < MOBA_TOPK

    def update(carry, s, vj):
        m, l, acc = carry
        m_new = jnp.maximum(m, jnp.max(s, axis=-1, keepdims=True))
        alpha = jnp.exp(m - m_new)
        p = jnp.exp(s - m_new)
        l = l * alpha + jnp.sum(p, axis=-1, keepdims=True)
        acc = acc * alpha + _dot(p.astype(BF16), vj)
        return m_new, l, acc

    own = pl.multiple_of(i * BLK, BLK)
    row = lax.broadcasted_iota(jnp.int32, (BLK, BLK), 0)
    col = lax.broadcasted_iota(jnp.int32, (BLK, BLK), 1)
    q16 = qn.astype(BF16)
    s_own = jnp.where(col <= row, _dot_nt(q16, kn_ref[pl.ds(own, BLK), :]) * scale, -jnp.inf)
    m0 = jnp.max(s_own, axis=-1, keepdims=True)
    p0 = jnp.exp(s_own - m0)
    carry =(m0, jnp.sum(p0, axis=-1, keepdims=True), _dot(p0.astype(BF16), v_ref[0, pl.ds(own, BLK), :]))

    for n in range(n_blk - 1):
        def visit(carry=carry, n=n):
            rows = slice(n * BLK, (n + 1) * BLK)
            s = _dot_nt(q16, kn_ref[rows, :]) * scale
            s = jnp.where(selected(n), s, -jnp.inf)
            return update(carry, s, v_ref[0, rows, :])
        carry = lax.cond(n < i, visit, lambda carry=carry: carry)

    m, l, acc = carry
    o_ref[0] = (acc / l).astype(o_ref.dtype)


def _moba(proj, q_norm_g, k_norm_g, *, heads, q_col, k_col, v_col):
    bsz, s, _ = proj.shape
    dh = ATT_HEAD_DIM
    BLK = MOBA_BLOCK
    n_blk = s // BLK
    assert s % BLK == 0 and n_blk <= LANES
    return pl.pallas_call(
        functools.partial(_moba_kernel, scale=dh ** -0.5, n_blk=n_blk),
        out_shape=jax.ShapeDtypeStruct((bsz, s, heads * dh), BF16),
        grid=(bsz, heads, n_blk),
        in_specs=[
            pl.BlockSpec((1, BLK, dh), lambda b, h, i: (b, i, q_col // dh + h)),
            pl.BlockSpec((1, s, dh), lambda b, h, i: (b, 0, k_col // dh + h)),
            pl.BlockSpec((1, s, dh), lambda b, h, i: (b, 0, v_col // dh + h)),
            pl.BlockSpec((1, dh), lambda b, h, i: (0, 0)),
            pl.BlockSpec((1, dh), lambda b, h, i: (0, 0)),
        ],
        out_specs=pl.BlockSpec((1, BLK, dh), lambda b, h, i: (b, i, h)),
        scratch_shapes=[
            pltpu.VMEM((s, dh), BF16),
            pltpu.VMEM((LANES, dh), BF16),
            pltpu.VMEM((LANES, dh), BF16),
        ],
        compiler_params=_params(("parallel", "parallel", "arbitrary"), 32),
        name="moba",
    )(proj, proj, proj, q_norm_g.reshape(1, dh), k_norm_g.reshape(1, dh))


def kernel(x, c, norm_mix_g, norm_ffn_g, ada_w, ada_b, ffn_w_gate, ffn_w_up, ffn_w_down, ab_w_in, ab_w_out, gm_ln_g, gm_ln_b, gm_w_s, gm_b_s, ssd_conv_w, ssd_conv_b, ssd_dt_bias, ssd_a_log, ssd_d, ssd_norm_g, cd_w_in, cd_w_out, moba_q_norm_g, moba_k_norm_g):
    bsz, s, d = x.shape
    depth = ada_w.shape[0]
    mod = _ada_mod(c, ada_w, ada_b)

    a_width = gm_ln_g.shape[1]
    b_width = ssd_norm_g.shape[1]
    conv_ch = ssd_conv_w.shape[2]
    ssd_heads = ssd_dt_bias.shape[1]
    ab_main = 2 * a_width + b_width + conv_ch
    n_att = cd_w_out.shape[1] // ATT_HEAD_DIM
    d_heads = n_att // 4
    c_heads = n_att - d_heads
    c_width = c_heads * ATT_HEAD_DIM
    d_width = d_heads * ATT_HEAD_DIM

    for layer in range(depth):
        sh_m, sc_m, g_m, sh_f, sc_f, g_f = [mod[layer, :, None, k * d:(k + 1) * d] for k in range(6)]
        if layer % 2 == 0:
            e = layer // 2
            w_in = ab_w_in[e]
            w_dt = jnp.pad(w_in[:, ab_main:], ((0, 0), (0, LANES - ssd_heads)))
            proj, dt_raw = _in_proj(x, norm_mix_g[layer], sc_m, sh_m, w_in[:, :ab_main].astype(BF16),
                                    n_act_cols=2 * a_width, w_extra=w_dt)
            y_a = _gmlp(proj, gm_ln_g[e], gm_ln_b[e], gm_w_s[e], gm_b_s[e], width=a_width)
            y_b = _ssd(proj, dt_raw, ssd_conv_w[e], ssd_conv_b[e], ssd_dt_bias[e], ssd_a_log[e], ssd_d[e],
                       ssd_norm_g[e], width=b_width, z_col=2 * a_width, xbc_col=2 * a_width + b_width)
            w_out = ab_w_out[e].astype(BF16)
            x = _out_proj(y_a, y_b, w_out[:a_width], w_out[a_width:], x, g_m)
        else:
            o = layer // 2
            proj = _in_proj(x, norm_mix_g[layer], sc_m, sh_m, cd_w_in[o].astype(BF16))
            y_c = _stick_breaking(proj, heads=c_heads, q_col=0, k_col=c_width, v_col=2 * c_width)
            y_d = _moba(proj, moba_q_norm_g[o], moba_k_norm_g[o], heads=d_heads,
                        q_col=3 * c_width, k_col=3 * c_width + d_width, v_col=3 * c_width + 2 * d_width)
            w_out = cd_w_out[o].astype(BF16)
            x = _out_proj(y_c, y_d, w_out[:c_width], w_out[c_width:], x, g_m)
        x = _ffn(x, norm_ffn_g[layer], sc_f, sh_f, g_f, ffn_w_gate[layer].astype(BF16),
                 ffn_w_up[layer].astype(BF16), ffn_w_down[layer].astype(BF16))
    return x
```

```python
import functools

import jax
import jax.numpy as jnp
from jax import lax
from jax.experimental import pallas as pl
from jax.experimental.pallas import tpu as pltpu

F32 = jnp.float32
BF16 = jnp.bfloat16

LANES = 128
A_CHUNK = 128
A_GROUP = 128
SSD_HEAD_DIM = 64
SSD_GROUPS = 4
SSD_STATE = 128
SSD_CONV = 4
SSD_CHUNK = 128
ATT_HEAD_DIM = 128
MOBA_BLOCK = 256
MOBA_TOPK = 3
SB_TILE = 256
MIB = 1 << 20


def _params(semantics, vmem_mib):
    return pltpu.CompilerParams(dimension_semantics=semantics, vmem_limit_bytes=vmem_mib * MIB)


def _split_bf16(v):
    hi = v.astype(BF16)
    lo = (v - hi.astype(F32)).astype(BF16)
    return hi, lo


def _split3_bf16(v):
    hi = v.astype(BF16)
    r = v - hi.astype(F32)
    mid = r.astype(BF16)
    lo = (r - mid.astype(F32)).astype(BF16)
    return hi, mid, lo


def _dot(a, b):
    return jnp.dot(a, b, preferred_element_type=F32)


def _dot_nt(a, b):
    return lax.dot_general(a, b, (((1,), (1,)), ((), ())), preferred_element_type=F32)


def _silu(v):
    return v * jax.nn.sigmoid(v)


def _ada_kernel(c_ref, w_ref, b_ref, o_ref):
    ca = _silu(c_ref[...])
    o_ref[0] = _dot(ca.astype(BF16), w_ref[0].astype(BF16)) + b_ref[0]


def _ada_mod(c, ada_w, ada_b):
    depth, d, n = ada_w.shape
    bsz = c.shape[0]
    tn = 1024
    return pl.pallas_call(
        _ada_kernel,
        out_shape=jax.ShapeDtypeStruct((depth, bsz, n), F32),
        grid=(depth, n // tn),
        in_specs=[
            pl.BlockSpec((bsz, d), lambda l, j: (0, 0)),
            pl.BlockSpec((1, d, tn), lambda l, j: (l, 0, j)),
            pl.BlockSpec((1, 1, tn), lambda l, j: (l, 0, j)),
        ],
        out_specs=pl.BlockSpec((1, bsz, tn), lambda l, j: (l, 0, j)),
        compiler_params=_params(("arbitrary", "arbitrary"), 40),
        name="ada_mod",
    )(c, ada_w, ada_b.reshape(depth, 1, n))


def _norm_mod(x, g, sc, sh):
    ms = jnp.mean(x * x, axis=-1, keepdims=True)
    y = x * lax.rsqrt(ms + 1e-6)
    return (y * g) * (1.0 + sc) + sh


def _inproj_kernel(*refs, n_act_tiles, has_extra):
    if has_extra:
        x_ref, g_ref, sc_ref, sh_ref, w_ref, wx_ref, o_ref, ox_ref, hm_ref = refs
    else:
        x_ref, g_ref, sc_ref, sh_ref, w_ref, o_ref, hm_ref = refs
    j = pl.program_id(2)

    @pl.when(j == 0)
    def _():
        h = _norm_mod(x_ref[0], g_ref[...], sc_ref[0], sh_ref[0])
        hi = h.astype(BF16)
        hm_ref[...] = hi
        if has_extra:
            lo = (h - hi.astype(F32)).astype(BF16)
            wx = wx_ref[...]
            ox_ref[0] = _dot(hi, wx) + _dot(lo, wx)

    acc = _dot(hm_ref[...], w_ref[...])
    if n_act_tiles:
        acc = jnp.where(j < n_act_tiles, jax.nn.gelu(acc), acc)
    o_ref[0] = acc.astype(o_ref.dtype)


def _in_proj(x, g, sc, sh, w, *, n_act_cols=0, w_extra=None, tm=1024, tn=1024):
    bsz, s, d = x.shape
    n = w.shape[1]
    tm = min(tm, s)
    has_extra = w_extra is not None
    in_specs = [
        pl.BlockSpec((1, tm, d), lambda b, i, j: (b, i, 0)),
        pl.BlockSpec((1, d), lambda b, i, j: (0, 0)),
        pl.BlockSpec((1, 1, d), lambda b, i, j: (b, 0, 0)),
        pl.BlockSpec((1, 1, d), lambda b, i, j: (b, 0, 0)),
        pl.BlockSpec((d, tn), lambda b, i, j: (0, j)),
    ]
    args = [x, g.reshape(1, d), sc, sh, w]
    out_shape = [jax.ShapeDtypeStruct((bsz, s, n), BF16)]
    out_specs = [pl.BlockSpec((1, tm, tn), lambda b, i, j: (b, i, j))]
    if has_extra:
        ne = w_extra.shape[1]
        assert 2 * ne <= LANES
        wxh, wxl = _split_bf16(w_extra)
        in_specs.append(pl.BlockSpec((d, LANES), lambda b, i, j: (0, 0)))
        args.append(jnp.pad(jnp.concatenate([wxh, wxl], axis=1), ((0, 0), (0, LANES - 2 * ne))))
        out_shape.append(jax.ShapeDtypeStruct((bsz, s, LANES), F32))
        out_specs.append(pl.BlockSpec((1, tm, LANES), lambda b, i, j: (b, i, 0)))
    res = pl.pallas_call(
        functools.partial(_inproj_kernel, n_act_tiles=n_act_cols // tn, has_extra=has_extra),
        out_shape=out_shape,
        grid=(bsz, s // tm, n // tn),
        in_specs=in_specs,
        out_specs=out_specs,
        scratch_shapes=[pltpu.VMEM((tm, d), BF16)],
        compiler_params=_params(("parallel", "parallel", "arbitrary"), 48),
        name="in_proj",
    )(*args)
    return res if has_extra else res[0]


def _outproj_kernel(ya_ref, yb_ref, wa_ref, wb_ref, x_ref, gate_ref, o_ref):
    acc = _dot(ya_ref[0], wa_ref[...]) + _dot(yb_ref[0], wb_ref[...])
    o_ref[0] = x_ref[0] + (1.0 + gate_ref[0]) * acc


def _out_proj(ya, yb, wa, wb, x, gate, *, tm=1024, tn=512):
    bsz, s, d = x.shape
    ka, kb = ya.shape[2], yb.shape[2]
    tm = min(tm, s)
    return pl.pallas_call(
        _outproj_kernel,
        out_shape=jax.ShapeDtypeStruct((bsz, s, d), F32),
        grid=(bsz, s // tm, d // tn),
        in_specs=[
            pl.BlockSpec((1, tm, ka), lambda b, i, j: (b, i, 0)),
            pl.BlockSpec((1, tm, kb), lambda b, i, j: (b, i, 0)),
            pl.BlockSpec((ka, tn), lambda b, i, j: (0, j)),
            pl.BlockSpec((kb, tn), lambda b, i, j: (0, j)),
            pl.BlockSpec((1, tm, tn), lambda b, i, j: (b, i, j)),
            pl.BlockSpec((1, 1, tn), lambda b, i, j: (b, 0, j)),
        ],
        out_specs=pl.BlockSpec((1, tm, tn), lambda b, i, j: (b, i, j)),
        compiler_params=_params(("parallel", "parallel", "arbitrary"), 48),
        name="out_proj",
    )(ya, yb, wa, wb, x, gate)


def _ffn_kernel(x_ref, g_ref, sc_ref, sh_ref, gate_ref, wg_ref, wu_ref, wd_ref, o_ref, hf_ref):
    f = pl.program_id(2)

    @pl.when(f == 0)
    def _():
        hf_ref[...] = _norm_mod(x_ref[0], g_ref[...], sc_ref[0], sh_ref[0]).astype(BF16)

    hf = hf_ref[...]
    a = _silu(_dot(hf, wg_ref[...])) * _dot(hf, wu_ref[...])
    part = _dot(a.astype(BF16), wd_ref[...])

    @pl.when(f == 0)
    def _():
        o_ref[0] = part

    @pl.when(f > 0)
    def _():
        o_ref[0] += part

    @pl.when(f == pl.num_programs(2) - 1)
    def _():
        o_ref[0] = x_ref[0] + (1.0 + gate_ref[0]) * o_ref[0]


def _ffn(x, g, sc, sh, gate, wg, wu, wd, *, tm=1024, tf=512):
    bsz, s, d = x.shape
    dff = wg.shape[1]
    tm = min(tm, s)
    vec = pl.BlockSpec((1, 1, d), lambda b, i, f: (b, 0, 0))
    once = pl.Buffered(1)
    return pl.pallas_call(
        _ffn_kernel,
        out_shape=jax.ShapeDtypeStruct((bsz, s, d), F32),
        grid=(bsz, s // tm, dff // tf),
        in_specs=[
            pl.BlockSpec((1, tm, d), lambda b, i, f: (b, i, 0), pipeline_mode=once),
            pl.BlockSpec((1, d), lambda b, i, f: (0, 0)),
            vec, vec, vec,
            pl.BlockSpec((d, tf), lambda b, i, f: (0, f)),
            pl.BlockSpec((d, tf), lambda b, i, f: (0, f)),
            pl.BlockSpec((tf, d), lambda b, i, f: (f, 0)),
        ],
        out_specs=pl.BlockSpec((1, tm, d), lambda b, i, f: (b, i, 0), pipeline_mode=once),
        scratch_shapes=[pltpu.VMEM((tm, d), BF16)],
        compiler_params=_params(("parallel", "parallel", "arbitrary"), 52),
        name="ffn",
    )(x, g.reshape(1, d), sc, sh, gate, wg, wu, wd)


def _gmlp_kernel(u_ref, v_ref, lng_ref, lnb_ref, ws_ref, bst_ref, o_ref, vln_ref):
    ts, width = v_ref.shape[1], v_ref.shape[2]
    v = v_ref[0].astype(F32)
    mu = jnp.mean(v, axis=-1, keepdims=True)
    vc = v - mu
    var = jnp.mean(vc * vc, axis=-1, keepdims=True)
    vln_ref[...] = (vc * lax.rsqrt(var + 1e-5) * lng_ref[...] + lnb_ref[...]).astype(BF16)
    row = lax.broadcasted_iota(jnp.int32, (A_CHUNK, A_CHUNK), 0)
    col = lax.broadcasted_iota(jnp.int32, (A_CHUNK, A_CHUNK), 1)
    causal = col <= row
    for h in range(width // A_GROUP):
        w = jnp.where(causal, ws_ref[h], 0.0).astype(BF16)
        bias = bst_ref[:, h:h + 1]
        cols = slice(h * A_GROUP, (h + 1) * A_GROUP)
        for c in range(ts // A_CHUNK):
            rows = slice(c * A_CHUNK, (c + 1) * A_CHUNK)
            mixed = _dot(w, vln_ref[rows, cols]) + bias
            o_ref[0, rows, cols] = (u_ref[0, rows, cols].astype(F32) * mixed).astype(o_ref.dtype)


def _gmlp(proj, ln_g, ln_b, w_s, b_s, *, width, ts=512):
    bsz, s, _ = proj.shape
    heads = w_s.shape[0]
    ts = min(ts, s)
    return pl.pallas_call(
        _gmlp_kernel,
        out_shape=jax.ShapeDtypeStruct((bsz, s, width), BF16),
        grid=(bsz, s // ts),
        in_specs=[
            pl.BlockSpec((1, ts, width), lambda b, i: (b, i, 0)),
            pl.BlockSpec((1, ts, width), lambda b, i: (b, i, 1)),
            pl.BlockSpec((1, width), lambda b, i: (0, 0)),
            pl.BlockSpec((1, width), lambda b, i: (0, 0)),
            pl.BlockSpec((heads, A_CHUNK, A_CHUNK), lambda b, i: (0, 0, 0)),
            pl.BlockSpec((A_CHUNK, heads), lambda b, i: (0, 0)),
        ],
        out_specs=pl.BlockSpec((1, ts, width), lambda b, i: (b, i, 0)),
        scratch_shapes=[pltpu.VMEM((ts, width), BF16)],
        compiler_params=_params(("parallel", "parallel"), 40),
        name="gmlp",
    )(proj, proj, ln_g.reshape(1, width), ln_b.reshape(1, width), w_s, b_s.T)


def _pair_bcast(mat, q, low_half):
    rows = mat.shape[0]
    a = jnp.broadcast_to(mat[:, 2 * q:2 * q + 1], (rows, LANES))
    b = jnp.broadcast_to(mat[:, 2 * q + 1:2 * q + 2], (rows, LANES))
    return jnp.where(low_half, a, b)


def _ssd_kernel(z_ref, xbc_ref, dt_ref, cw_ref, cb_ref, dtb_ref, alog_ref, dskip_ref, ng_ref,
                o_ref, xpad_ref, xs_ref, state_ref, y_ref, *, width, n_bc):
    L = SSD_CHUNK
    c = pl.program_id(1)
    pad = 8

    @pl.when(c == 0)
    def _():
        xpad_ref[0:pad, :] = jnp.zeros((pad, xpad_ref.shape[1]), F32)
        state_ref[...] = jnp.zeros_like(state_ref)

    @pl.when(c > 0)
    def _():
        xpad_ref[0:pad, :] = xpad_ref[L:L + pad, :]

    xpad_ref[pad:pad + L, :] = xbc_ref[0].astype(F32)

    conv = cb_ref[...] + cw_ref[0:1, :] * xpad_ref[pad - 3:pad - 3 + L, :]
    for k in range(1, SSD_CONV):
        conv = conv + cw_ref[k:k + 1, :] * xpad_ref[pad - 3 + k:pad - 3 + k + L, :]
    xs_ref[...] = _silu(conv)

    heads = width // SSD_HEAD_DIM
    dt_raw = dt_ref[0]
    dt_raw = dt_raw + pltpu.roll(dt_raw, LANES - heads, axis=1)
    dt = jax.nn.softplus(dt_raw + dtb_ref[...])
    a_dt = dt * (-jnp.exp(alog_ref[...]))
    row = lax.broadcasted_iota(jnp.int32, (L, L), 0)
    col = lax.broadcasted_iota(jnp.int32, (L, L), 1)
    causal = col <= row
    tri = jnp.where(causal, 1.0, 0.0).astype(BF16)
    p0, p1, p2 = _split3_bf16(a_dt)
    a_cum = _dot(tri, p0) + _dot(tri, p1) + _dot(tri, p2)
    a_cum_t = a_cum.T
    a_last = a_cum[L - 1:L, :]
    e_cum = jnp.exp(a_cum)
    e_end = jnp.exp(a_last - a_cum)

    low_half = lax.broadcasted_iota(jnp.int32, (L, LANES), 1) < SSD_HEAD_DIM
    heads_per_group = width // SSD_HEAD_DIM // SSD_GROUPS
    pairs_per_group = heads_per_group // 2
    gw = width // SSD_GROUPS
    for g in range(SSD_GROUPS):
        bm = xs_ref[:, width + g * SSD_STATE:width + (g + 1) * SSD_STATE]
        cm = xs_ref[:, width + n_bc + g * SSD_STATE:width + n_bc + (g + 1) * SSD_STATE]
        bm16 = bm.astype(BF16)
        cm16 = cm.astype(BF16)
        cb = _dot_nt(cm16, bm16)
        y_off_g = _dot(cm16, state_ref[:, g * gw:(g + 1) * gw].astype(BF16))
        bm_t16 = bm.T.astype(BF16)
        for pq in range(pairs_per_group):
            q = g * pairs_per_group + pq
            lanes = slice(q * LANES, (q + 1) * LANES)
            xs_p = xs_ref[:, lanes]
            xc = xs_p * _pair_bcast(dt, q, low_half)
            ms = []
            for r in (2 * q, 2 * q + 1):
                seg = a_cum[:, r:r + 1] - a_cum_t[r:r + 1, :]
                decay = jnp.exp(jnp.where(causal, seg, -jnp.inf))
                ms.append((cb * decay).astype(BF16))
            xc16 = xc.astype(BF16)
            zero = jnp.zeros_like(xc16)
            rhs = jnp.concatenate([jnp.where(low_half, xc16, zero), jnp.where(low_half, zero, xc16)], axis=0)
            y_diag = _dot(jnp.concatenate(ms, axis=1), rhs)
            e_p = _pair_bcast(e_cum, q, low_half)
            y_p = y_diag + y_off_g[:, pq * LANES:(pq + 1) * LANES] * e_p + dskip_ref[:, lanes] * xs_p
            y_ref[:, lanes] = y_p
            xdec16 = (xc * _pair_bcast(e_end, q, low_half)).astype(BF16)
            s_new = _dot(bm_t16, xdec16)
            state_ref[:, lanes] = state_ref[:, lanes] * e_p[L - 1:L, :] + s_new

    z = z_ref[0].astype(F32)
    yz = y_ref[...] * _silu(z)
    ms = jnp.mean(yz * yz, axis=-1, keepdims=True)
    o_ref[0] = (yz * lax.rsqrt(ms + 1e-6) * ng_ref[...]).astype(o_ref.dtype)


def _ssd(proj, dt_raw, conv_w, conv_b, dt_bias, a_log, d_skip, norm_g, *, width, z_col, xbc_col):
    bsz, s, _ = proj.shape
    L = SSD_CHUNK
    conv_ch = conv_w.shape[1]
    n_bc = SSD_GROUPS * SSD_STATE
    heads = width // SSD_HEAD_DIM
    assert conv_ch == width + 2 * n_bc and z_col % width == 0 and xbc_col % conv_ch == 0

    def padl(v):
        return jnp.pad(v.astype(F32), (0, LANES - heads)).reshape(1, LANES)

    const = lambda shape: pl.BlockSpec(shape, lambda b, c: (0,) * len(shape))
    return pl.pallas_call(
        functools.partial(_ssd_kernel, width=width, n_bc=n_bc),
        out_shape=jax.ShapeDtypeStruct((bsz, s, width), BF16),
        grid=(bsz, s // L),
        in_specs=[
            pl.BlockSpec((1, L, width), lambda b, c: (b, c, z_col // width)),
            pl.BlockSpec((1, L, conv_ch), lambda b, c: (b, c, xbc_col // conv_ch)),
            pl.BlockSpec((1, L, LANES), lambda b, c: (b, c, 0)),
            const((SSD_CONV, conv_ch)),
            const((1, conv_ch)),
            const((1, LANES)),
            const((1, LANES)),
            const((1, width)),
            const((1, width)),
        ],
        out_specs=pl.BlockSpec((1, L, width), lambda b, c: (b, c, 0)),
        scratch_shapes=[
            pltpu.VMEM((L + 8, conv_ch), F32),
            pltpu.VMEM((L, conv_ch), F32),
            pltpu.VMEM((SSD_STATE, width), F32),
            pltpu.VMEM((L, width), F32),
        ],
        compiler_params=_params(("parallel", "arbitrary"), 40),
        name="ssd",
    )(proj, proj, dt_raw, conv_w, conv_b.reshape(1, conv_ch), padl(dt_bias), padl(a_log),
      jnp.repeat(d_skip.astype(F32), SSD_HEAD_DIM).reshape(1, width), norm_g.reshape(1, width))


def _sb_kernel(q_ref, k_ref, v_ref, o_ref, *, heads):
    T = SB_TILE
    dh = ATT_HEAD_DIM
    i = pl.program_id(2)
    row = lax.broadcasted_iota(jnp.int32, (T, T), 0)
    col = lax.broadcasted_iota(jnp.int32, (T, T), 1)
    past = col < row
    suffix = jnp.where(row >= col, 1.0, 0.0).astype(BF16)
    suffix2 = jnp.concatenate([suffix, suffix], axis=0)

    def tile(h, start, carry, acc, diagonal):
        cols = slice(h * dh, (h + 1) * dh)
        logits = _dot_nt(q_ref[0, :, cols], k_ref[0, pl.ds(start, T), cols])
        log_keep = -(jnp.maximum(logits, 0.0) + jnp.log(1.0 + jnp.exp(-jnp.abs(logits))))
        if diagonal:
            log_keep = jnp.where(past, log_keep, 0.0)
        hi, lo = _split_bf16(log_keep)
        rinc = _dot(jnp.concatenate([hi, lo], axis=1), suffix2)
        w = jnp.exp(logits + rinc + carry)
        if diagonal:
            w = jnp.where(past, w, 0.0)
        acc = acc + _dot(w.astype(BF16), v_ref[0, pl.ds(start, T), cols])
        return carry + rinc[:, 0:1], acc

    def step(j, state, diagonal):
        start = pl.multiple_of(j * T, T)
        return tuple(tile(h, start, *state[h], diagonal) for h in range(heads))

    zero = (jnp.zeros((T, 1), F32), jnp.zeros((T, dh), F32))
    state = step(i, (zero,) * heads, True)
    state = lax.fori_loop(0, i, lambda n, st: step(i - 1 - n, st, False), state)
    for h in range(heads):
        o_ref[0, :, h * dh:(h + 1) * dh] = state[h][1].astype(o_ref.dtype)


def _stick_breaking(proj, *, heads, q_col, k_col, v_col, heads_per_step=4):
    bsz, s, _ = proj.shape
    T = SB_TILE
    hb = heads_per_step
    wb = hb * ATT_HEAD_DIM
    assert heads % hb == 0 and q_col % wb == 0 and k_col % wb == 0 and v_col % wb == 0 and s % T == 0
    return pl.pallas_call(
        functools.partial(_sb_kernel, heads=hb),
        out_shape=jax.ShapeDtypeStruct((bsz, s, heads * ATT_HEAD_DIM), BF16),
        grid=(bsz, heads // hb, s // T),
        in_specs=[
            pl.BlockSpec((1, T, wb), lambda b, h, i: (b, i, q_col // wb + h)),
            pl.BlockSpec((1, s, wb), lambda b, h, i: (b, 0, k_col // wb + h)),
            pl.BlockSpec((1, s, wb), lambda b, h, i: (b, 0, v_col // wb + h)),
        ],
        out_specs=pl.BlockSpec((1, T, wb), lambda b, h, i: (b, i, h)),
        compiler_params=_params(("parallel", "parallel", "arbitrary"), 40),
        name="stick_breaking",
    )(proj, proj, proj)


def _head_rms(v, g):
    ms = jnp.mean(v * v, axis=-1, keepdims=True)
    return v * lax.rsqrt(ms + 1e-6) * g


def _moba_kernel(q_ref, k_ref, v_ref, qg_ref, kg_ref, o_ref, kn_ref, kmh_ref, kml_ref, *, scale, n_blk):
    BLK = MOBA_BLOCK
    i = pl.program_id(2)

    @pl.when(i == 0)
    def _():
        kmh_ref[...] = jnp.zeros_like(kmh_ref)
        kml_ref[...] = jnp.zeros_like(kml_ref)
        for n in range(n_blk):
            rows = slice(n * BLK, (n + 1) * BLK)
            kn = _head_rms(k_ref[0, rows, :].astype(F32), kg_ref[...])
            kn_ref[rows, :] = kn.astype(BF16)
            hi, lo = _split_bf16(jnp.mean(kn, axis=0, keepdims=True))
            kmh_ref[n:n + 1, :] = hi
            kml_ref[n:n + 1, :] = lo

    qn = _head_rms(q_ref[0].astype(F32), qg_ref[...])
    q16 = (qn * scale).astype(BF16)
    lane = lax.broadcasted_iota(jnp.int32, (BLK, LANES), 1)
    row = lax.broadcasted_iota(jnp.int32, (BLK, BLK), 0)
    col = lax.broadcasted_iota(jnp.int32, (BLK, BLK), 1)

    def q_block(iv):
        rows = lambda n: slice(n * BLK, (n + 1) * BLK)
        scores = [jnp.where(col <= row, _dot_nt(q16, kn_ref[rows(iv), :]), -jnp.inf)]
        if iv > MOBA_TOPK:
            qh, ql = _split_bf16(qn)
            kmh = kmh_ref[...]
            gate = _dot_nt(qh, kmh) + _dot_nt(ql, kmh) + _dot_nt(qh, kml_ref[...])
            gate = jnp.where(lane < iv, gate, -jnp.inf)
        for n in range(iv):
            s = _dot_nt(q16, kn_ref[rows(n), :])
            if iv > MOBA_TOPK:
                gn = gate[:, n:n + 1]
                ahead = (gate > gn) | ((gate == gn) & (lane < n))
                rank = jnp.sum(jnp.where(ahead, 1.0, 0.0), axis=-1, keepdims=True)
                s = jnp.where(rank < MOBA_TOPK, s, -jnp.inf)
            scores.append(s)
        m = functools.reduce(jnp.maximum, [jnp.max(s, axis=-1, keepdims=True) for s in scores])
        l = 0.0
        acc = 0.0
        for s, n in zip(scores, [iv] + list(range(iv))):
            p = jnp.exp(s - m)
            l = l + jnp.sum(p, axis=-1, keepdims=True)
            acc = acc + _dot(p.astype(BF16), v_ref[0, rows(n), :])
        o_ref[0] = (acc / l).astype(o_ref.dtype)

    for iv in range(n_blk):
        pl.when(i == iv)(functools.partial(q_block, iv))


def _moba(proj, q_norm_g, k_norm_g, *, heads, q_col, k_col, v_col):
    bsz, s, _ = proj.shape
    dh = ATT_HEAD_DIM
    BLK = MOBA_BLOCK
    n_blk = s // BLK
    assert s % BLK == 0 and n_blk <= LANES
    return pl.pallas_call(
        functools.partial(_moba_kernel, scale=dh ** -0.5, n_blk=n_blk),
        out_shape=jax.ShapeDtypeStruct((bsz, s, heads * dh), BF16),
        grid=(bsz, heads, n_blk),
        in_specs=[
            pl.BlockSpec((1, BLK, dh), lambda b, h, i: (b, i, q_col // dh + h)),
            pl.BlockSpec((1, s, dh), lambda b, h, i: (b, 0, k_col // dh + h)),
            pl.BlockSpec((1, s, dh), lambda b, h, i: (b, 0, v_col // dh + h)),
            pl.BlockSpec((1, dh), lambda b, h, i: (0, 0)),
            pl.BlockSpec((1, dh), lambda b, h, i: (0, 0)),
        ],
        out_specs=pl.BlockSpec((1, BLK, dh), lambda b, h, i: (b, i, h)),
        scratch_shapes=[
            pltpu.VMEM((s, dh), BF16),
            pltpu.VMEM((LANES, dh), BF16),
            pltpu.VMEM((LANES, dh), BF16),
        ],
        compiler_params=_params(("parallel", "parallel", "arbitrary"), 32),
        name="moba",
    )(proj, proj, proj, q_norm_g.reshape(1, dh), k_norm_g.reshape(1, dh))


def kernel(x, c, norm_mix_g, norm_ffn_g, ada_w, ada_b, ffn_w_gate, ffn_w_up, ffn_w_down, ab_w_in, ab_w_out, gm_ln_g, gm_ln_b, gm_w_s, gm_b_s, ssd_conv_w, ssd_conv_b, ssd_dt_bias, ssd_a_log, ssd_d, ssd_norm_g, cd_w_in, cd_w_out, moba_q_norm_g, moba_k_norm_g):
    bsz, s, d = x.shape
    depth = ada_w.shape[0]
    mod = _ada_mod(c, ada_w, ada_b)

    a_width = gm_ln_g.shape[1]
    b_width = ssd_norm_g.shape[1]
    conv_ch = ssd_conv_w.shape[2]
    ssd_heads = ssd_dt_bias.shape[1]
    ab_main = 2 * a_width + b_width + conv_ch
    n_att = cd_w_out.shape[1] // ATT_HEAD_DIM
    d_heads = n_att // 4
    c_heads = n_att - d_heads
    c_width = c_heads * ATT_HEAD_DIM
    d_width = d_heads * ATT_HEAD_DIM

    for layer in range(depth):
        sh_m, sc_m, g_m, sh_f, sc_f, g_f = [mod[layer, :, None, k * d:(k + 1) * d] for k in range(6)]
        if layer % 2 == 0:
            e = layer // 2
            w_in = ab_w_in[e]
            proj, dt_raw = _in_proj(x, norm_mix_g[layer], sc_m, sh_m, w_in[:, :ab_main].astype(BF16),
                                    n_act_cols=2 * a_width, w_extra=w_in[:, ab_main:])
            y_a = _gmlp(proj, gm_ln_g[e], gm_ln_b[e], gm_w_s[e], gm_b_s[e], width=a_width)
            y_b = _ssd(proj, dt_raw, ssd_conv_w[e], ssd_conv_b[e], ssd_dt_bias[e], ssd_a_log[e], ssd_d[e],
                       ssd_norm_g[e], width=b_width, z_col=2 * a_width, xbc_col=2 * a_width + b_width)
            w_out = ab_w_out[e].astype(BF16)
            x = _out_proj(y_a, y_b, w_out[:a_width], w_out[a_width:], x, g_m)
        else:
            o = layer // 2
            col_scale = jnp.where(jnp.arange(cd_w_in.shape[2]) < c_width, ATT_HEAD_DIM ** -0.5, 1.0)
            proj = _in_proj(x, norm_mix_g[layer], sc_m, sh_m, (cd_w_in[o] * col_scale).astype(BF16))
            y_c = _stick_breaking(proj, heads=c_heads, q_col=0, k_col=c_width, v_col=2 * c_width)
            y_d = _moba(proj, moba_q_norm_g[o], moba_k_norm_g[o], heads=d_heads,
                        q_col=3 * c_width, k_col=3 * c_width + d_width, v_col=3 * c_width + 2 * d_width)
            w_out = cd_w_out[o].astype(BF16)
            x = _out_proj(y_c, y_d, w_out[:c_width], w_out[c_width:], x, g_m)
        x = _ffn(x, norm_ffn_g[layer], sc_f, sh_f, g_f, ffn_w_gate[layer].astype(BF16),
                 ffn_w_up[layer].astype(BF16), ffn_w_down[layer].astype(BF16))
    return x
```

```python
import functools

import jax
import jax.numpy as jnp
from jax import lax
from jax.experimental import pallas as pl
from jax.experimental.pallas import tpu as pltpu

F32 = jnp.float32
BF16 = jnp.bfloat16

LANES = 128
A_CHUNK = 128
A_GROUP = 128
SSD_HEAD_DIM = 64
SSD_GROUPS = 4
SSD_STATE = 128
SSD_CONV = 4
SSD_CHUNK = 128
ATT_HEAD_DIM = 128
MOBA_BLOCK = 256
MOBA_TOPK = 3
SB_TILE = 256
NORM_ROWS = 64
LOG2E = 1.4426950408889634
MIB = 1 << 20


def _params(semantics, vmem_mib):
    return pltpu.CompilerParams(dimension_semantics=semantics, vmem_limit_bytes=vmem_mib * MIB)


def _split_bf16(v):
    hi = v.astype(BF16)
    lo = (v - hi.astype(F32)).astype(BF16)
    return hi, lo


def _split3_bf16(v):
    hi = v.astype(BF16)
    r = v - hi.astype(F32)
    mid = r.astype(BF16)
    lo = (r - mid.astype(F32)).astype(BF16)
    return hi, mid, lo


def _dot(a, b):
    return jnp.dot(a, b, preferred_element_type=F32)


def _dot_nt(a, b):
    return lax.dot_general(a, b, (((1,), (1,)), ((), ())), preferred_element_type=F32)


def _silu(v):
    return v * (0.5 * jnp.tanh(0.5 * v) + 0.5)


def _ada_kernel(c_ref, w_ref, b_ref, o_ref):
    ca = _silu(c_ref[...])
    o_ref[0] = _dot(ca.astype(BF16), w_ref[0].astype(BF16)) + b_ref[0]


def _ada_mod(c, ada_w, ada_b):
    depth, d, n = ada_w.shape
    bsz = c.shape[0]
    tn = 1024
    return pl.pallas_call(
        _ada_kernel,
        out_shape=jax.ShapeDtypeStruct((depth, bsz, n), F32),
        grid=(depth, n // tn),
        in_specs=[
            pl.BlockSpec((bsz, d), lambda l, j: (0, 0)),
            pl.BlockSpec((1, d, tn), lambda l, j: (l, 0, j)),
            pl.BlockSpec((1, 1, tn), lambda l, j: (l, 0, j)),
        ],
        out_specs=pl.BlockSpec((1, bsz, tn), lambda l, j: (l, 0, j)),
        compiler_params=_params(("arbitrary", "arbitrary"), 40),
        name="ada_mod",
    )(c, ada_w, ada_b.reshape(depth, 1, n))


def _norm_mod_store(x_ref, g_ref, sc_ref, sh_ref, hi_ref, lo_ref=None):
    scale = g_ref[...] * (1.0 + sc_ref[0])
    shift = sh_ref[0]

    def body(r, carry):
        rows = pl.ds(pl.multiple_of(r * NORM_ROWS, NORM_ROWS), NORM_ROWS)
        x = x_ref[0, rows, :]
        rs = lax.rsqrt(jnp.mean(x * x, axis=-1, keepdims=True) + 1e-6)
        h = x * rs * scale + shift
        hi = h.astype(BF16)
        hi_ref[rows, :] = hi
        if lo_ref is not None:
            lo_ref[rows, :] = (h - hi.astype(F32)).astype(BF16)
        return carry

    lax.fori_loop(0, hi_ref.shape[0] // NORM_ROWS, body, 0)


def _inproj_kernel(*refs, n_act_tiles, has_extra):
    if has_extra:
        x_ref, g_ref, sc_ref, sh_ref, w_ref, wx_ref, o_ref, ox_ref, hm_ref, lo_ref = refs
    else:
        x_ref, g_ref, sc_ref, sh_ref, w_ref, o_ref, hm_ref = refs
        lo_ref = None
    j = pl.program_id(2)

    @pl.when(j == 0)
    def _():
        _norm_mod_store(x_ref, g_ref, sc_ref, sh_ref, hm_ref, lo_ref)
        if has_extra:
            wx = wx_ref[...]
            ox_ref[0] = _dot(hm_ref[...], wx) + _dot(lo_ref[...], wx)

    acc = _dot(hm_ref[...], w_ref[...])
    if n_act_tiles:
        acc = jnp.where(j < n_act_tiles, jax.nn.gelu(acc), acc)
    o_ref[0] = acc.astype(o_ref.dtype)


def _in_proj(x, g, sc, sh, w, *, n_cols=None, n_act_cols=0, w_extra=None, tm=1024, tn=1024):
    bsz, s, d = x.shape
    n = w.shape[1] if n_cols is None else n_cols
    tm = min(tm, s)
    assert n % tn == 0 and n_act_cols % tn == 0 and tm % NORM_ROWS == 0
    has_extra = w_extra is not None
    in_specs = [
        pl.BlockSpec((1, tm, d), lambda b, i, j: (b, i, 0)),
        pl.BlockSpec((1, d), lambda b, i, j: (0, 0)),
        pl.BlockSpec((1, 1, d), lambda b, i, j: (b, 0, 0)),
        pl.BlockSpec((1, 1, d), lambda b, i, j: (b, 0, 0)),
        pl.BlockSpec((d, tn), lambda b, i, j: (0, j)),
    ]
    args = [x, g.reshape(1, d), sc, sh, w]
    out_shape = [jax.ShapeDtypeStruct((bsz, s, n), BF16)]
    out_specs = [pl.BlockSpec((1, tm, tn), lambda b, i, j: (b, i, j))]
    if has_extra:
        ne = w_extra.shape[1]
        assert 2 * ne <= LANES
        wxh, wxl = _split_bf16(w_extra)
        in_specs.append(pl.BlockSpec((d, LANES), lambda b, i, j: (0, 0)))
        args.append(jnp.pad(jnp.concatenate([wxh, wxl], axis=1), ((0, 0), (0, LANES - 2 * ne))))
        out_shape.append(jax.ShapeDtypeStruct((bsz, s, LANES), F32))
        out_specs.append(pl.BlockSpec((1, tm, LANES), lambda b, i, j: (b, i, 0)))
    res = pl.pallas_call(
        functools.partial(_inproj_kernel, n_act_tiles=n_act_cols // tn, has_extra=has_extra),
        out_shape=out_shape,
        grid=(bsz, s // tm, n // tn),
        in_specs=in_specs,
        out_specs=out_specs,
        scratch_shapes=[pltpu.VMEM((tm, d), BF16)] * (2 if has_extra else 1),
        compiler_params=_params(("parallel", "parallel", "arbitrary"), 48),
        name="in_proj",
    )(*args)
    return res if has_extra else res[0]


def _outproj_kernel(ya_ref, yb_ref, wa_ref, wb_ref, x_ref, gate_ref, o_ref):
    acc = _dot(ya_ref[0], wa_ref[...]) + _dot(yb_ref[0], wb_ref[...])
    o_ref[0] = x_ref[0] + (1.0 + gate_ref[0]) * acc


def _out_proj(ya, yb, w, x, gate, *, tm=1024, tn=512):
    bsz, s, d = x.shape
    ka, kb = ya.shape[2], yb.shape[2]
    tm = min(tm, s)
    assert w.shape[0] == ka + kb and ka % kb == 0
    return pl.pallas_call(
        _outproj_kernel,
        out_shape=jax.ShapeDtypeStruct((bsz, s, d), F32),
        grid=(bsz, s // tm, d // tn),
        in_specs=[
            pl.BlockSpec((1, tm, ka), lambda b, i, j: (b, i, 0)),
            pl.BlockSpec((1, tm, kb), lambda b, i, j: (b, i, 0)),
            pl.BlockSpec((ka, tn), lambda b, i, j: (0, j)),
            pl.BlockSpec((kb, tn), lambda b, i, j: (ka // kb, j)),
            pl.BlockSpec((1, tm, tn), lambda b, i, j: (b, i, j)),
            pl.BlockSpec((1, 1, tn), lambda b, i, j: (b, 0, j)),
        ],
        out_specs=pl.BlockSpec((1, tm, tn), lambda b, i, j: (b, i, j)),
        compiler_params=_params(("parallel", "parallel", "arbitrary"), 48),
        name="out_proj",
    )(ya, yb, w, w, x, gate)


def _ffn_kernel(x_ref, g_ref, sc_ref, sh_ref, gate_ref, wg_ref, wu_ref, wd_ref, o_ref, hf_ref):
    f = pl.program_id(2)

    @pl.when(f == 0)
    def _():
        _norm_mod_store(x_ref, g_ref, sc_ref, sh_ref, hf_ref)
        o_ref[0] = jnp.zeros(o_ref.shape[1:], F32)

    hf = hf_ref[...]
    a = _silu(_dot(hf, wg_ref[0])) * _dot(hf, wu_ref[0])
    o_ref[0] += _dot(a.astype(BF16), wd_ref[0])

    @pl.when(f == pl.num_programs(2) - 1)
    def _():
        o_ref[0] = x_ref[0] + (1.0 + gate_ref[0]) * o_ref[0]


def _ffn(x, g, sc, sh, gate, wg, wu, wd, layer, *, tm=512, tf=512):
    bsz, s, d = x.shape
    dff = wg.shape[2]
    tm = min(tm, s)
    assert dff % tf == 0 and tm % NORM_ROWS == 0
    vec = pl.BlockSpec((1, 1, d), lambda b, i, f: (b, 0, 0))
    return pl.pallas_call(
        _ffn_kernel,
        out_shape=jax.ShapeDtypeStruct((bsz, s, d), F32),
        grid=(bsz, s // tm, dff // tf),
        in_specs=[
            pl.BlockSpec((1, tm, d), lambda b, i, f: (b, i, 0)),
            pl.BlockSpec((1, d), lambda b, i, f: (0, 0)),
            vec, vec, vec,
            pl.BlockSpec((1, d, tf), lambda b, i, f: (layer, 0, f)),
            pl.BlockSpec((1, d, tf), lambda b, i, f: (layer, 0, f)),
            pl.BlockSpec((1, tf, d), lambda b, i, f: (layer, f, 0)),
        ],
        out_specs=pl.BlockSpec((1, tm, d), lambda b, i, f: (b, i, 0)),
        scratch_shapes=[pltpu.VMEM((tm, d), BF16)],
        compiler_params=_params(("parallel", "parallel", "arbitrary"), 48),
        name="ffn",
    )(x, g.reshape(1, d), sc, sh, gate, wg, wu, wd)


def _gmlp_kernel(u_ref, v_ref, lng_ref, lnb_ref, ws_ref, bst_ref, o_ref, vln_ref):
    ts, width = v_ref.shape[1], v_ref.shape[2]
    v = v_ref[0].astype(F32)
    mu = jnp.mean(v, axis=-1, keepdims=True)
    vc = v - mu
    var = jnp.mean(vc * vc, axis=-1, keepdims=True)
    vln_ref[...] = (vc * lax.rsqrt(var + 1e-5) * lng_ref[...] + lnb_ref[...]).astype(BF16)
    row = lax.broadcasted_iota(jnp.int32, (A_CHUNK, A_CHUNK), 0)
    col = lax.broadcasted_iota(jnp.int32, (A_CHUNK, A_CHUNK), 1)
    causal = col <= row
    for h in range(width // A_GROUP):
        w = jnp.where(causal, ws_ref[h], 0.0).astype(BF16)
        bias = bst_ref[:, h:h + 1]
        cols = slice(h * A_GROUP, (h + 1) * A_GROUP)
        for c in range(ts // A_CHUNK):
            rows = slice(c * A_CHUNK, (c + 1) * A_CHUNK)
            mixed = _dot(w, vln_ref[rows, cols]) + bias
            o_ref[0, rows, cols] = (u_ref[0, rows, cols].astype(F32) * mixed).astype(o_ref.dtype)


def _gmlp(proj, ln_g, ln_b, w_s, b_s, *, width, ts=512):
    bsz, s, _ = proj.shape
    heads = w_s.shape[0]
    ts = min(ts, s)
    return pl.pallas_call(
        _gmlp_kernel,
        out_shape=jax.ShapeDtypeStruct((bsz, s, width), BF16),
        grid=(bsz, s // ts),
        in_specs=[
            pl.BlockSpec((1, ts, width), lambda b, i: (b, i, 0)),
            pl.BlockSpec((1, ts, width), lambda b, i: (b, i, 1)),
            pl.BlockSpec((1, width), lambda b, i: (0, 0)),
            pl.BlockSpec((1, width), lambda b, i: (0, 0)),
            pl.BlockSpec((heads, A_CHUNK, A_CHUNK), lambda b, i: (0, 0, 0)),
            pl.BlockSpec((A_CHUNK, heads), lambda b, i: (0, 0)),
        ],
        out_specs=pl.BlockSpec((1, ts, width), lambda b, i: (b, i, 0)),
        scratch_shapes=[pltpu.VMEM((ts, width), BF16)],
        compiler_params=_params(("parallel", "parallel"), 40),
        name="gmlp",
    )(proj, proj, ln_g.reshape(1, width), ln_b.reshape(1, width), w_s, b_s.T)


def _pair_bcast(mat, q, low_half):
    rows = mat.shape[0]
    a = jnp.broadcast_to(mat[:, 2 * q:2 * q + 1], (rows, LANES))
    b = jnp.broadcast_to(mat[:, 2 * q + 1:2 * q + 2], (rows, LANES))
    return jnp.where(low_half, a, b)


def _ssd_kernel(z_ref, xbc_ref, dt_ref, cw_ref, cb_ref, dtb_ref, alog_ref, dskip_ref, ng_ref,
                o_ref, xpad_ref, xs_ref, state_ref, y_ref, *, width, n_bc):
    L = SSD_CHUNK
    c = pl.program_id(1)
    pad = 8

    @pl.when(c == 0)
    def _():
        xpad_ref[0:pad, :] = jnp.zeros((pad, xpad_ref.shape[1]), F32)
        state_ref[...] = jnp.zeros_like(state_ref)

    @pl.when(c > 0)
    def _():
        xpad_ref[0:pad, :] = xpad_ref[L:L + pad, :]

    xpad_ref[pad:pad + L, :] = xbc_ref[0].astype(F32)

    conv = cb_ref[...] + cw_ref[0:1, :] * xpad_ref[pad - 3:pad - 3 + L, :]
    for k in range(1, SSD_CONV):
        conv = conv + cw_ref[k:k + 1, :] * xpad_ref[pad - 3 + k:pad - 3 + k + L, :]
    xs_ref[...] = _silu(conv)

    heads = width // SSD_HEAD_DIM
    dt_raw = dt_ref[0]
    dt_raw = dt_raw + pltpu.roll(dt_raw, LANES - heads, axis=1)
    dt = jax.nn.softplus(dt_raw + dtb_ref[...])
    a_dt = dt * (-jnp.exp(alog_ref[...]))
    row = lax.broadcasted_iota(jnp.int32, (L, L), 0)
    col = lax.broadcasted_iota(jnp.int32, (L, L), 1)
    causal = col <= row
    tri = jnp.where(causal, 1.0, 0.0).astype(BF16)
    p0, p1, p2 = _split3_bf16(a_dt)
    a_cum = _dot(tri, p0) + _dot(tri, p1) + _dot(tri, p2)
    a_cum_t = a_cum.T
    a_last = a_cum[L - 1:L, :]
    dt_t = dt.T
    e_cum = jnp.exp(a_cum)
    dt_end = dt * jnp.exp(a_last - a_cum)

    low_half = lax.broadcasted_iota(jnp.int32, (L, LANES), 1) < SSD_HEAD_DIM
    heads_per_group = width // SSD_HEAD_DIM // SSD_GROUPS
    pairs_per_group = heads_per_group // 2
    gw = width // SSD_GROUPS
    for g in range(SSD_GROUPS):
        bm = xs_ref[:, width + g * SSD_STATE:width + (g + 1) * SSD_STATE]
        cm = xs_ref[:, width + n_bc + g * SSD_STATE:width + n_bc + (g + 1) * SSD_STATE]
        bm16 = bm.astype(BF16)
        cm16 = cm.astype(BF16)
        cb = _dot_nt(cm16, bm16)
        y_off_g = _dot(cm16, state_ref[:, g * gw:(g + 1) * gw].astype(BF16))
        bm_t16 = bm.T.astype(BF16)
        for pq in range(pairs_per_group):
            q = g * pairs_per_group + pq
            lanes = slice(q * LANES, (q + 1) * LANES)
            xs_p = xs_ref[:, lanes]
            ms = []
            for r in (2 * q, 2 * q + 1):
                seg = a_cum[:, r:r + 1] - a_cum_t[r:r + 1, :]
                decay = jnp.exp(jnp.where(causal, seg, -jnp.inf))
                ms.append((cb * decay * dt_t[r:r + 1, :]).astype(BF16))
            xs16 = xs_p.astype(BF16)
            zero = jnp.zeros_like(xs16)
            rhs = jnp.concatenate([jnp.where(low_half, xs16, zero), jnp.where(low_half, zero, xs16)], axis=0)
            y_diag = _dot(jnp.concatenate(ms, axis=1), rhs)
            e_p = _pair_bcast(e_cum, q, low_half)
            y_p = y_diag + y_off_g[:, pq * LANES:(pq + 1) * LANES] * e_p + dskip_ref[:, lanes] * xs_p
            y_ref[:, lanes] = y_p
            xdec16 = (xs_p * _pair_bcast(dt_end, q, low_half)).astype(BF16)
            s_new = _dot(bm_t16, xdec16)
            state_ref[:, lanes] = state_ref[:, lanes] * e_p[L - 1:L, :] + s_new

    z = z_ref[0].astype(F32)
    yz = y_ref[...] * _silu(z)
    ms = jnp.mean(yz * yz, axis=-1, keepdims=True)
    o_ref[0] = (yz * lax.rsqrt(ms + 1e-6) * ng_ref[...]).astype(o_ref.dtype)


def _ssd(proj, dt_raw, conv_w, conv_b, dt_bias, a_log, d_skip, norm_g, *, width, z_col, xbc_col):
    bsz, s, _ = proj.shape
    L = SSD_CHUNK
    conv_ch = conv_w.shape[1]
    n_bc = SSD_GROUPS * SSD_STATE
    heads = width // SSD_HEAD_DIM
    assert conv_ch == width + 2 * n_bc and z_col % width == 0 and xbc_col % conv_ch == 0

    def padl(v):
        return jnp.pad(v.astype(F32), (0, LANES - heads)).reshape(1, LANES)

    const = lambda shape: pl.BlockSpec(shape, lambda b, c: (0,) * len(shape))
    return pl.pallas_call(
        functools.partial(_ssd_kernel, width=width, n_bc=n_bc),
        out_shape=jax.ShapeDtypeStruct((bsz, s, width), BF16),
        grid=(bsz, s // L),
        in_specs=[
            pl.BlockSpec((1, L, width), lambda b, c: (b, c, z_col // width)),
            pl.BlockSpec((1, L, conv_ch), lambda b, c: (b, c, xbc_col // conv_ch)),
            pl.BlockSpec((1, L, LANES), lambda b, c: (b, c, 0)),
            const((SSD_CONV, conv_ch)),
            const((1, conv_ch)),
            const((1, LANES)),
            const((1, LANES)),
            const((1, width)),
            const((1, width)),
        ],
        out_specs=pl.BlockSpec((1, L, width), lambda b, c: (b, c, 0)),
        scratch_shapes=[
            pltpu.VMEM((L + 8, conv_ch), F32),
            pltpu.VMEM((L, conv_ch), F32),
            pltpu.VMEM((SSD_STATE, width), F32),
            pltpu.VMEM((L, width), F32),
        ],
        compiler_params=_params(("parallel", "arbitrary"), 40),
        name="ssd",
    )(proj, proj, dt_raw, conv_w, conv_b.reshape(1, conv_ch), padl(dt_bias), padl(a_log),
      jnp.repeat(d_skip.astype(F32), SSD_HEAD_DIM).reshape(1, width), norm_g.reshape(1, width))


def _sb_kernel(q_ref, k_ref, v_ref, o_ref, *, heads):
    T = SB_TILE
    dh = ATT_HEAD_DIM
    i = pl.program_id(2)
    row = lax.broadcasted_iota(jnp.int32, (T, T), 0)
    col = lax.broadcasted_iota(jnp.int32, (T, T), 1)
    past = col < row
    suffix = jnp.where(row >= col, 1.0, 0.0).astype(BF16)
    suffix2 = jnp.concatenate([suffix, suffix], axis=0)

    def step(j, state, diagonal):
        start = pl.multiple_of(j * T, T)
        cols = [slice(h * dh, (h + 1) * dh) for h in range(heads)]
        logits = [_dot_nt(q_ref[0, :, c], k_ref[0, pl.ds(start, T), c]) for c in cols]
        cost = [jnp.maximum(x, 0.0) + jnp.log2(1.0 + jnp.exp2(-jnp.abs(x))) for x in logits]
        if diagonal:
            cost = [jnp.where(past, c, 0.0) for c in cost]
        stacked = [jnp.concatenate(_split_bf16(c), axis=1) for c in cost]
        rinc = [_dot(s, suffix2) for s in stacked]
        w = [jnp.exp2(x - r - st[0]) for x, r, st in zip(logits, rinc, state)]
        if diagonal:
            w = [jnp.where(past, wh, 0.0) for wh in w]
        acc = [st[1] + _dot(wh.astype(BF16), v_ref[0, pl.ds(start, T), c]) for wh, st, c in zip(w, state, cols)]
        return tuple((st[0] + r[:, 0:1], a) for st, r, a in zip(state, rinc, acc))

    zero = (jnp.zeros((T, 1), F32), jnp.zeros((T, dh), F32))
    state = step(i, (zero,) * heads, True)
    state = lax.fori_loop(0, i, lambda n, st: step(i - 1 - n, st, False), state)
    for h in range(heads):
        o_ref[0, :, h * dh:(h + 1) * dh] = state[h][1].astype(o_ref.dtype)


def _stick_breaking(proj, *, heads, q_col, k_col, v_col, heads_per_step=6):
    bsz, s, _ = proj.shape
    T = SB_TILE
    hb = heads_per_step
    wb = hb * ATT_HEAD_DIM
    assert heads % hb == 0 and q_col % wb == 0 and k_col % wb == 0 and v_col % wb == 0 and s % T == 0
    return pl.pallas_call(
        functools.partial(_sb_kernel, heads=hb),
        out_shape=jax.ShapeDtypeStruct((bsz, s, heads * ATT_HEAD_DIM), BF16),
        grid=(bsz, heads // hb, s // T),
        in_specs=[
            pl.BlockSpec((1, T, wb), lambda b, h, i: (b, i, q_col // wb + h)),
            pl.BlockSpec((1, s, wb), lambda b, h, i: (b, 0, k_col // wb + h)),
            pl.BlockSpec((1, s, wb), lambda b, h, i: (b, 0, v_col // wb + h)),
        ],
        out_specs=pl.BlockSpec((1, T, wb), lambda b, h, i: (b, i, h)),
        compiler_params=_params(("parallel", "parallel", "arbitrary"), 40),
        name="stick_breaking",
    )(proj, proj, proj)


def _head_rms(v, g):
    ms = jnp.mean(v * v, axis=-1, keepdims=True)
    return v * lax.rsqrt(ms + 1e-6) * g


def _moba_kernel(q_ref, k_ref, v_ref, qg_ref, kg_ref, o_ref, kn_ref, kmh_ref, kml_ref, *, scale, n_blk):
    BLK = MOBA_BLOCK
    i = pl.program_id(2)

    @pl.when(i == 0)
    def _():
        kmh_ref[...] = jnp.zeros_like(kmh_ref)
        kml_ref[...] = jnp.zeros_like(kml_ref)
        for n in range(n_blk):
            rows = slice(n * BLK, (n + 1) * BLK)
            kn = _head_rms(k_ref[0, rows, :].astype(F32), kg_ref[...])
            kn_ref[rows, :] = kn.astype(BF16)
            hi, lo = _split_bf16(jnp.mean(kn, axis=0, keepdims=True))
            kmh_ref[n:n + 1, :] = hi
            kml_ref[n:n + 1, :] = lo

    qn = _head_rms(q_ref[0].astype(F32), qg_ref[...])
    q16 = (qn * scale).astype(BF16)
    lane = lax.broadcasted_iota(jnp.int32, (BLK, LANES), 1)
    row = lax.broadcasted_iota(jnp.int32, (BLK, BLK), 0)
    col = lax.broadcasted_iota(jnp.int32, (BLK, BLK), 1)

    def q_block(iv):
        rows = lambda n: slice(n * BLK, (n + 1) * BLK)
        scores = [jnp.where(col <= row, _dot_nt(q16, kn_ref[rows(iv), :]), -jnp.inf)]
        if iv > MOBA_TOPK:
            qh, ql = _split_bf16(qn)
            kmh = kmh_ref[...]
            gate = _dot_nt(qh, kmh) + _dot_nt(ql, kmh) + _dot_nt(qh, kml_ref[...])
            gate = jnp.where(lane < iv, gate, -jnp.inf)
        for n in range(iv):
            s = _dot_nt(q16, kn_ref[rows(n), :])
            if iv > MOBA_TOPK:
                gn = gate[:, n:n + 1]
                ahead = (gate > gn) | ((gate == gn) & (lane < n))
                rank = jnp.sum(jnp.where(ahead, 1.0, 0.0), axis=-1, keepdims=True)
                s = jnp.where(rank < MOBA_TOPK, s, -jnp.inf)
            scores.append(s)
        m = functools.reduce(jnp.maximum, [jnp.max(s, axis=-1, keepdims=True) for s in scores])
        l = 0.0
        acc = 0.0
        for s, n in zip(scores, [iv] + list(range(iv))):
            p = jnp.exp(s - m)
            l = l + jnp.sum(p, axis=-1, keepdims=True)
            acc = acc + _dot(p.astype(BF16), v_ref[0, rows(n), :])
        o_ref[0] = (acc / l).astype(o_ref.dtype)

    for iv in range(n_blk):
        pl.when(i == iv)(functools.partial(q_block, iv))


def _moba(proj, q_norm_g, k_norm_g, *, heads, q_col, k_col, v_col):
    bsz, s, _ = proj.shape
    dh = ATT_HEAD_DIM
    BLK = MOBA_BLOCK
    n_blk = s // BLK
    assert s % BLK == 0 and n_blk <= LANES
    return pl.pallas_call(
        functools.partial(_moba_kernel, scale=dh ** -0.5, n_blk=n_blk),
        out_shape=jax.ShapeDtypeStruct((bsz, s, heads * dh), BF16),
        grid=(bsz, heads, n_blk),
        in_specs=[
            pl.BlockSpec((1, BLK, dh), lambda b, h, i: (b, i, q_col // dh + h)),
            pl.BlockSpec((1, s, dh), lambda b, h, i: (b, 0, k_col // dh + h)),
            pl.BlockSpec((1, s, dh), lambda b, h, i: (b, 0, v_col // dh + h)),
            pl.BlockSpec((1, dh), lambda b, h, i: (0, 0)),
            pl.BlockSpec((1, dh), lambda b, h, i: (0, 0)),
        ],
        out_specs=pl.BlockSpec((1, BLK, dh), lambda b, h, i: (b, i, h)),
        scratch_shapes=[
            pltpu.VMEM((s, dh), BF16),
            pltpu.VMEM((LANES, dh), BF16),
            pltpu.VMEM((LANES, dh), BF16),
        ],
        compiler_params=_params(("parallel", "parallel", "arbitrary"), 32),
        name="moba",
    )(proj, proj, proj, q_norm_g.reshape(1, dh), k_norm_g.reshape(1, dh))


def kernel(x, c, norm_mix_g, norm_ffn_g, ada_w, ada_b, ffn_w_gate, ffn_w_up, ffn_w_down, ab_w_in, ab_w_out, gm_ln_g, gm_ln_b, gm_w_s, gm_b_s, ssd_conv_w, ssd_conv_b, ssd_dt_bias, ssd_a_log, ssd_d, ssd_norm_g, cd_w_in, cd_w_out, moba_q_norm_g, moba_k_norm_g):
    bsz, s, d = x.shape
    depth = ada_w.shape[0]
    mod = _ada_mod(c, ada_w, ada_b)

    a_width = gm_ln_g.shape[1]
    b_width = ssd_norm_g.shape[1]
    conv_ch = ssd_conv_w.shape[2]
    ssd_heads = ssd_dt_bias.shape[1]
    ab_main = 2 * a_width + b_width + conv_ch
    n_att = cd_w_out.shape[1] // ATT_HEAD_DIM
    d_heads = n_att // 4
    c_heads = n_att - d_heads
    c_width = c_heads * ATT_HEAD_DIM
    d_width = d_heads * ATT_HEAD_DIM

    wg16, wu16, wd16 = ffn_w_gate.astype(BF16), ffn_w_up.astype(BF16), ffn_w_down.astype(BF16)
    for layer in range(depth):
        sh_m, sc_m, g_m, sh_f, sc_f, g_f = [mod[layer, :, None, k * d:(k + 1) * d] for k in range(6)]
        if layer % 2 == 0:
            e = layer // 2
            w_in = ab_w_in[e]
            proj, dt_raw = _in_proj(x, norm_mix_g[layer], sc_m, sh_m, w_in.astype(BF16), n_cols=ab_main,
                                    n_act_cols=2 * a_width, w_extra=w_in[:, ab_main:])
            y_a = _gmlp(proj, gm_ln_g[e], gm_ln_b[e], gm_w_s[e], gm_b_s[e], width=a_width)
            y_b = _ssd(proj, dt_raw, ssd_conv_w[e], ssd_conv_b[e], ssd_dt_bias[e], ssd_a_log[e], ssd_d[e],
                       ssd_norm_g[e], width=b_width, z_col=2 * a_width, xbc_col=2 * a_width + b_width)
            x = _out_proj(y_a, y_b, ab_w_out[e].astype(BF16), x, g_m)
        else:
            o = layer // 2
            col_scale = jnp.where(jnp.arange(cd_w_in.shape[2]) < c_width, ATT_HEAD_DIM ** -0.5 * LOG2E, 1.0)
            proj = _in_proj(x, norm_mix_g[layer], sc_m, sh_m, (cd_w_in[o] * col_scale).astype(BF16))
            y_c = _stick_breaking(proj, heads=c_heads, q_col=0, k_col=c_width, v_col=2 * c_width)
            y_d = _moba(proj, moba_q_norm_g[o], moba_k_norm_g[o], heads=d_heads,
                        q_col=3 * c_width, k_col=3 * c_width + d_width, v_col=3 * c_width + 2 * d_width)
            x = _out_proj(y_c, y_d, cd_w_out[o].astype(BF16), x, g_m)
        x = _ffn(x, norm_ffn_g[layer], sc_f, sh_f, g_f, wg16, wu16, wd16, layer)
    return x
```

```python
import functools

import jax
import jax.numpy as jnp
from jax import lax
from jax.experimental import pallas as pl
from jax.experimental.pallas import tpu as pltpu

F32 = jnp.float32
BF16 = jnp.bfloat16

LANES = 128
A_CHUNK = 128
A_GROUP = 128
SSD_HEAD_DIM = 64
SSD_GROUPS = 4
SSD_STATE = 128
SSD_CONV = 4
SSD_CHUNK = 128
ATT_HEAD_DIM = 128
MOBA_BLOCK = 256
MOBA_TOPK = 3
SB_TILE = 256
NORM_ROWS = 64
LOG2E = 1.4426950408889634
MIB = 1 << 20


def _params(semantics, vmem_mib):
    return pltpu.CompilerParams(dimension_semantics=semantics, vmem_limit_bytes=vmem_mib * MIB)


def _split_bf16(v):
    hi = v.astype(BF16)
    lo = (v - hi.astype(F32)).astype(BF16)
    return hi, lo


def _split3_bf16(v):
    hi = v.astype(BF16)
    r = v - hi.astype(F32)
    mid = r.astype(BF16)
    lo = (r - mid.astype(F32)).astype(BF16)
    return hi, mid, lo


def _dot(a, b):
    return jnp.dot(a, b, preferred_element_type=F32)


def _dot_nt(a, b):
    return lax.dot_general(a, b, (((1,), (1,)), ((), ())), preferred_element_type=F32)


def _silu(v):
    return v * (0.5 * jnp.tanh(0.5 * v) + 0.5)


def _ada_kernel(c_ref, w_ref, b_ref, o_ref):
    ca = _silu(c_ref[...])
    o_ref[0] = _dot(ca.astype(BF16), w_ref[0].astype(BF16)) + b_ref[0]


def _ada_mod(c, ada_w, ada_b):
    depth, d, n = ada_w.shape
    bsz = c.shape[0]
    tn = 1024
    return pl.pallas_call(
        _ada_kernel,
        out_shape=jax.ShapeDtypeStruct((depth, bsz, n), F32),
        grid=(depth, n // tn),
        in_specs=[
            pl.BlockSpec((bsz, d), lambda l, j: (0, 0)),
            pl.BlockSpec((1, d, tn), lambda l, j: (l, 0, j)),
            pl.BlockSpec((1, 1, tn), lambda l, j: (l, 0, j)),
        ],
        out_specs=pl.BlockSpec((1, bsz, tn), lambda l, j: (l, 0, j)),
        compiler_params=_params(("arbitrary", "arbitrary"), 40),
        name="ada_mod",
    )(c, ada_w, ada_b.reshape(depth, 1, n))


def _norm_mod_store(x_ref, g_ref, sc_ref, sh_ref, h_ref):
    scale = g_ref[...] * (1.0 + sc_ref[0])
    shift = sh_ref[0]

    def body(r, carry):
        rows = pl.ds(pl.multiple_of(r * NORM_ROWS, NORM_ROWS), NORM_ROWS)
        x = x_ref[0, rows, :]
        rs = lax.rsqrt(jnp.mean(x * x, axis=-1, keepdims=True) + 1e-6)
        h_ref[rows, :] = (x * rs * scale + shift).astype(BF16)
        return carry

    lax.fori_loop(0, h_ref.shape[0] // NORM_ROWS, body, 0)


def _inproj_kernel(*refs, n_act_tiles, has_extra, w_transposed):
    if has_extra:
        x_ref, g_ref, sc_ref, sh_ref, w_ref, wx_ref, o_ref, ox_ref, hm_ref = refs
    else:
        x_ref, g_ref, sc_ref, sh_ref, w_ref, o_ref, hm_ref = refs
    j = pl.program_id(2)
    mm = _dot_nt if w_transposed else _dot

    @pl.when(j == 0)
    def _():
        _norm_mod_store(x_ref, g_ref, sc_ref, sh_ref, hm_ref)
        if has_extra:
            ox_ref[0] = mm(hm_ref[...], wx_ref[...])

    def plain():
        o_ref[0] = mm(hm_ref[...], w_ref[...]).astype(o_ref.dtype)

    def activated():
        o_ref[0] = jax.nn.gelu(mm(hm_ref[...], w_ref[...])).astype(o_ref.dtype)

    if n_act_tiles:
        pl.when(j < n_act_tiles)(activated)
        pl.when(j >= n_act_tiles)(plain)
    else:
        plain()


def _in_proj(x, g, sc, sh, w, *, n_cols=None, n_act_cols=0, w_extra=None, w_transposed=False, tm=1024, tn=1024):
    bsz, s, d = x.shape
    n = (w.shape[0] if w_transposed else w.shape[1]) if n_cols is None else n_cols
    tm = min(tm, s)
    assert n % tn == 0 and n_act_cols % tn == 0 and tm % NORM_ROWS == 0
    has_extra = w_extra is not None
    in_specs = [
        pl.BlockSpec((1, tm, d), lambda b, i, j: (b, i, 0)),
        pl.BlockSpec((1, d), lambda b, i, j: (0, 0)),
        pl.BlockSpec((1, 1, d), lambda b, i, j: (b, 0, 0)),
        pl.BlockSpec((1, 1, d), lambda b, i, j: (b, 0, 0)),
        pl.BlockSpec((tn, d), lambda b, i, j: (j, 0)) if w_transposed else pl.BlockSpec((d, tn), lambda b, i, j: (0, j)),
    ]
    args = [x, g.reshape(1, d), sc, sh, w]
    out_shape = [jax.ShapeDtypeStruct((bsz, s, n), BF16)]
    out_specs = [pl.BlockSpec((1, tm, tn), lambda b, i, j: (b, i, j))]
    if has_extra:
        axis = 0 if w_transposed else 1
        ne = w_extra.shape[axis]
        assert 2 * ne <= LANES
        pad = [(0, 0), (0, 0)]
        pad[axis] = (0, LANES - 2 * ne)
        wx = jnp.pad(jnp.concatenate(_split_bf16(w_extra), axis=axis), pad)
        in_specs.append(pl.BlockSpec(wx.shape, lambda b, i, j: (0, 0)))
        args.append(wx)
        out_shape.append(jax.ShapeDtypeStruct((bsz, s, LANES), F32))
        out_specs.append(pl.BlockSpec((1, tm, LANES), lambda b, i, j: (b, i, 0)))
    res = pl.pallas_call(
        functools.partial(_inproj_kernel, n_act_tiles=n_act_cols // tn, has_extra=has_extra,
                          w_transposed=w_transposed),
        out_shape=out_shape,
        grid=(bsz, s // tm, n // tn),
        in_specs=in_specs,
        out_specs=out_specs,
        scratch_shapes=[pltpu.VMEM((tm, d), BF16)],
        compiler_params=_params(("parallel", "parallel", "arbitrary"), 48),
        name="in_proj",
    )(*args)
    return res if has_extra else res[0]


def _outproj_kernel(ya_ref, yb_ref, wa_ref, wb_ref, x_ref, gate_ref, o_ref):
    acc = _dot(ya_ref[0], wa_ref[...]) + _dot(yb_ref[0], wb_ref[...])
    o_ref[0] = x_ref[0] + (1.0 + gate_ref[0]) * acc


def _out_proj(ya, yb, w, x, gate, *, tm=1024, tn=512):
    bsz, s, d = x.shape
    ka, kb = ya.shape[2], yb.shape[2]
    tm = min(tm, s)
    assert w.shape[0] == ka + kb and ka % kb == 0
    return pl.pallas_call(
        _outproj_kernel,
        out_shape=jax.ShapeDtypeStruct((bsz, s, d), F32),
        grid=(bsz, s // tm, d // tn),
        in_specs=[
            pl.BlockSpec((1, tm, ka), lambda b, i, j: (b, i, 0)),
            pl.BlockSpec((1, tm, kb), lambda b, i, j: (b, i, 0)),
            pl.BlockSpec((ka, tn), lambda b, i, j: (0, j)),
            pl.BlockSpec((kb, tn), lambda b, i, j: (ka // kb, j)),
            pl.BlockSpec((1, tm, tn), lambda b, i, j: (b, i, j)),
            pl.BlockSpec((1, 1, tn), lambda b, i, j: (b, 0, j)),
        ],
        out_specs=pl.BlockSpec((1, tm, tn), lambda b, i, j: (b, i, j)),
        compiler_params=_params(("parallel", "parallel", "arbitrary"), 48),
        name="out_proj",
    )(ya, yb, w, w, x, gate)


def _ffn_kernel(x_ref, g_ref, sc_ref, sh_ref, gate_ref, wg_ref, wu_ref, wd_ref, o_ref, hf_ref):
    f = pl.program_id(2)

    @pl.when(f == 0)
    def _():
        _norm_mod_store(x_ref, g_ref, sc_ref, sh_ref, hf_ref)
        o_ref[0] = jnp.zeros(o_ref.shape[1:], F32)

    hf = hf_ref[...]
    a = _silu(_dot(hf, wg_ref[0])) * _dot(hf, wu_ref[0])
    o_ref[0] += _dot(a.astype(BF16), wd_ref[0])

    @pl.when(f == pl.num_programs(2) - 1)
    def _():
        o_ref[0] = x_ref[0] + (1.0 + gate_ref[0]) * o_ref[0]


def _ffn(x, g, sc, sh, gate, wg, wu, wd, layer, *, tm=512, tf=512):
    bsz, s, d = x.shape
    dff = wg.shape[2]
    tm = min(tm, s)
    assert dff % tf == 0 and tm % NORM_ROWS == 0
    vec = pl.BlockSpec((1, 1, d), lambda b, i, f: (b, 0, 0))
    return pl.pallas_call(
        _ffn_kernel,
        out_shape=jax.ShapeDtypeStruct((bsz, s, d), F32),
        grid=(bsz, s // tm, dff // tf),
        in_specs=[
            pl.BlockSpec((1, tm, d), lambda b, i, f: (b, i, 0)),
            pl.BlockSpec((1, d), lambda b, i, f: (0, 0)),
            vec, vec, vec,
            pl.BlockSpec((1, d, tf), lambda b, i, f: (layer, 0, f)),
            pl.BlockSpec((1, d, tf), lambda b, i, f: (layer, 0, f)),
            pl.BlockSpec((1, tf, d), lambda b, i, f: (layer, f, 0)),
        ],
        out_specs=pl.BlockSpec((1, tm, d), lambda b, i, f: (b, i, 0)),
        scratch_shapes=[pltpu.VMEM((tm, d), BF16)],
        compiler_params=_params(("parallel", "parallel", "arbitrary"), 48),
        name="ffn",
    )(x, g.reshape(1, d), sc, sh, gate, wg, wu, wd)


def _gmlp_kernel(u_ref, v_ref, lng_ref, lnb_ref, ws_ref, bst_ref, o_ref, vln_ref):
    ts, width = v_ref.shape[1], v_ref.shape[2]
    v = v_ref[0].astype(F32)
    mu = jnp.mean(v, axis=-1, keepdims=True)
    vc = v - mu
    var = jnp.mean(vc * vc, axis=-1, keepdims=True)
    vln_ref[...] = (vc * lax.rsqrt(var + 1e-5) * lng_ref[...] + lnb_ref[...]).astype(BF16)
    row = lax.broadcasted_iota(jnp.int32, (A_CHUNK, A_CHUNK), 0)
    col = lax.broadcasted_iota(jnp.int32, (A_CHUNK, A_CHUNK), 1)
    causal = col <= row
    for h in range(width // A_GROUP):
        w = jnp.where(causal, ws_ref[h], 0.0).astype(BF16)
        bias = bst_ref[:, h:h + 1]
        cols = slice(h * A_GROUP, (h + 1) * A_GROUP)
        for c in range(ts // A_CHUNK):
            rows = slice(c * A_CHUNK, (c + 1) * A_CHUNK)
            mixed = _dot(w, vln_ref[rows, cols]) + bias
            o_ref[0, rows, cols] = (u_ref[0, rows, cols].astype(F32) * mixed).astype(o_ref.dtype)


def _gmlp(proj, ln_g, ln_b, w_s, b_s, *, width, ts=512):
    bsz, s, _ = proj.shape
    heads = w_s.shape[0]
    ts = min(ts, s)
    return pl.pallas_call(
        _gmlp_kernel,
        out_shape=jax.ShapeDtypeStruct((bsz, s, width), BF16),
        grid=(bsz, s // ts),
        in_specs=[
            pl.BlockSpec((1, ts, width), lambda b, i: (b, i, 0)),
            pl.BlockSpec((1, ts, width), lambda b, i: (b, i, 1)),
            pl.BlockSpec((1, width), lambda b, i: (0, 0)),
            pl.BlockSpec((1, width), lambda b, i: (0, 0)),
            pl.BlockSpec((heads, A_CHUNK, A_CHUNK), lambda b, i: (0, 0, 0)),
            pl.BlockSpec((A_CHUNK, heads), lambda b, i: (0, 0)),
        ],
        out_specs=pl.BlockSpec((1, ts, width), lambda b, i: (b, i, 0)),
        scratch_shapes=[pltpu.VMEM((ts, width), BF16)],
        compiler_params=_params(("parallel", "parallel"), 40),
        name="gmlp",
    )(proj, proj, ln_g.reshape(1, width), ln_b.reshape(1, width), w_s, b_s.T)


def _pair_bcast(mat, q, low_half):
    rows = mat.shape[0]
    a = jnp.broadcast_to(mat[:, 2 * q:2 * q + 1], (rows, LANES))
    b = jnp.broadcast_to(mat[:, 2 * q + 1:2 * q + 2], (rows, LANES))
    return jnp.where(low_half, a, b)


def _ssd_kernel(z_ref, xbc_ref, dt_ref, cw_ref, cb_ref, dtb_ref, alog_ref, dskip_ref, ng_ref,
                o_ref, xpad_ref, xs_ref, state_ref, y_ref, *, width, n_bc):
    L = SSD_CHUNK
    c = pl.program_id(1)
    pad = 8

    @pl.when(c == 0)
    def _():
        xpad_ref[0:pad, :] = jnp.zeros((pad, xpad_ref.shape[1]), F32)
        state_ref[...] = jnp.zeros_like(state_ref)

    @pl.when(c > 0)
    def _():
        xpad_ref[0:pad, :] = xpad_ref[L:L + pad, :]

    xpad_ref[pad:pad + L, :] = xbc_ref[0].astype(F32)

    conv = cb_ref[...] + cw_ref[0:1, :] * xpad_ref[pad - 3:pad - 3 + L, :]
    for k in range(1, SSD_CONV):
        conv = conv + cw_ref[k:k + 1, :] * xpad_ref[pad - 3 + k:pad - 3 + k + L, :]
    xs_ref[...] = _silu(conv)

    heads = width // SSD_HEAD_DIM
    dt_raw = dt_ref[0]
    dt_raw = dt_raw + pltpu.roll(dt_raw, LANES - heads, axis=1)
    dt = jax.nn.softplus(dt_raw + dtb_ref[...])
    a_dt = dt * (-jnp.exp(alog_ref[...]))
    row = lax.broadcasted_iota(jnp.int32, (L, L), 0)
    col = lax.broadcasted_iota(jnp.int32, (L, L), 1)
    causal = col <= row
    tri = jnp.where(causal, 1.0, 0.0).astype(BF16)
    p0, p1, p2 = _split3_bf16(a_dt)
    a_cum = _dot(tri, p0) + _dot(tri, p1) + _dot(tri, p2)
    a_cum_t = a_cum.T
    a_last = a_cum[L - 1:L, :]
    dt_t = dt.T
    e_cum = jnp.exp(a_cum)
    dt_end = dt * jnp.exp(a_last - a_cum)

    low_half = lax.broadcasted_iota(jnp.int32, (L, LANES), 1) < SSD_HEAD_DIM
    heads_per_group = width // SSD_HEAD_DIM // SSD_GROUPS
    pairs_per_group = heads_per_group // 2
    gw = width // SSD_GROUPS
    for g in range(SSD_GROUPS):
        bm = xs_ref[:, width + g * SSD_STATE:width + (g + 1) * SSD_STATE]
        cm = xs_ref[:, width + n_bc + g * SSD_STATE:width + n_bc + (g + 1) * SSD_STATE]
        bm16 = bm.astype(BF16)
        cm16 = cm.astype(BF16)
        cb = _dot_nt(cm16, bm16)
        y_off_g = _dot(cm16, state_ref[:, g * gw:(g + 1) * gw].astype(BF16))
        bm_t16 = bm.T.astype(BF16)
        for pq in range(pairs_per_group):
            q = g * pairs_per_group + pq
            lanes = slice(q * LANES, (q + 1) * LANES)
            xs_p = xs_ref[:, lanes]
            ms = []
            for r in (2 * q, 2 * q + 1):
                seg = a_cum[:, r:r + 1] - a_cum_t[r:r + 1, :]
                decay = jnp.exp(jnp.where(causal, seg, -jnp.inf))
                ms.append((cb * decay * dt_t[r:r + 1, :]).astype(BF16))
            xs16 = xs_p.astype(BF16)
            zero = jnp.zeros_like(xs16)
            rhs = jnp.concatenate([jnp.where(low_half, xs16, zero), jnp.where(low_half, zero, xs16)], axis=0)
            y_diag = _dot(jnp.concatenate(ms, axis=1), rhs)
            e_p = _pair_bcast(e_cum, q, low_half)
            y_p = y_diag + y_off_g[:, pq * LANES:(pq + 1) * LANES] * e_p + dskip_ref[:, lanes] * xs_p
            y_ref[:, lanes] = y_p
            xdec16 = (xs_p * _pair_bcast(dt_end, q, low_half)).astype(BF16)
            s_new = _dot(bm_t16, xdec16)
            state_ref[:, lanes] = state_ref[:, lanes] * e_p[L - 1:L, :] + s_new

    z = z_ref[0].astype(F32)
    yz = y_ref[...] * _silu(z)
    ms = jnp.mean(yz * yz, axis=-1, keepdims=True)
    o_ref[0] = (yz * lax.rsqrt(ms + 1e-6) * ng_ref[...]).astype(o_ref.dtype)


def _ssd(proj, dt_raw, conv_w, conv_b, dt_bias, a_log, d_skip, norm_g, *, width, z_col, xbc_col):
    bsz, s, _ = proj.shape
    L = SSD_CHUNK
    conv_ch = conv_w.shape[1]
    n_bc = SSD_GROUPS * SSD_STATE
    heads = width // SSD_HEAD_DIM
    assert conv_ch == width + 2 * n_bc and z_col % width == 0 and xbc_col % conv_ch == 0

    def padl(v):
        return jnp.pad(v.astype(F32), (0, LANES - heads)).reshape(1, LANES)

    const = lambda shape: pl.BlockSpec(shape, lambda b, c: (0,) * len(shape))
    return pl.pallas_call(
        functools.partial(_ssd_kernel, width=width, n_bc=n_bc),
        out_shape=jax.ShapeDtypeStruct((bsz, s, width), BF16),
        grid=(bsz, s // L),
        in_specs=[
            pl.BlockSpec((1, L, width), lambda b, c: (b, c, z_col // width)),
            pl.BlockSpec((1, L, conv_ch), lambda b, c: (b, c, xbc_col // conv_ch)),
            pl.BlockSpec((1, L, LANES), lambda b, c: (b, c, 0)),
            const((SSD_CONV, conv_ch)),
            const((1, conv_ch)),
            const((1, LANES)),
            const((1, LANES)),
            const((1, width)),
            const((1, width)),
        ],
        out_specs=pl.BlockSpec((1, L, width), lambda b, c: (b, c, 0)),
        scratch_shapes=[
            pltpu.VMEM((L + 8, conv_ch), F32),
            pltpu.VMEM((L, conv_ch), F32),
            pltpu.VMEM((SSD_STATE, width), F32),
            pltpu.VMEM((L, width), F32),
        ],
        compiler_params=_params(("parallel", "arbitrary"), 40),
        name="ssd",
    )(proj, proj, dt_raw, conv_w, conv_b.reshape(1, conv_ch), padl(dt_bias), padl(a_log),
      jnp.repeat(d_skip.astype(F32), SSD_HEAD_DIM).reshape(1, width), norm_g.reshape(1, width))


def _sb_kernel(q_ref, k_ref, v_ref, o_ref, *, heads):
    T = SB_TILE
    dh = ATT_HEAD_DIM
    i = pl.program_id(2)
    row = lax.broadcasted_iota(jnp.int32, (T, T), 0)
    col = lax.broadcasted_iota(jnp.int32, (T, T), 1)
    past = col < row
    suffix = jnp.where(row >= col, 1.0, 0.0).astype(BF16)

    def step(j, state, diagonal):
        start = pl.multiple_of(j * T, T)
        cols = [slice(h * dh, (h + 1) * dh) for h in range(heads)]
        logits = [_dot_nt(q_ref[0, :, c], k_ref[0, pl.ds(start, T), c]) for c in cols]
        cost = [jnp.maximum(x, 0.0) + jnp.log2(1.0 + jnp.exp2(-jnp.abs(x))) for x in logits]
        if diagonal:
            cost = [jnp.where(past, c, 0.0) for c in cost]
        rinc = [_dot(c.astype(BF16), suffix) for c in cost]
        w = [jnp.exp2(x - r - st[0]) for x, r, st in zip(logits, rinc, state)]
        if diagonal:
            w = [jnp.where(past, wh, 0.0) for wh in w]
        acc = [st[1] + _dot(wh.astype(BF16), v_ref[0, pl.ds(start, T), c]) for wh, st, c in zip(w, state, cols)]
        return tuple((st[0] + r[:, 0:1], a) for st, r, a in zip(state, rinc, acc))

    zero = (jnp.zeros((T, 1), F32), jnp.zeros((T, dh), F32))
    state = step(i, (zero,) * heads, True)
    state = lax.fori_loop(0, i, lambda n, st: step(i - 1 - n, st, False), state)
    for h in range(heads):
        o_ref[0, :, h * dh:(h + 1) * dh] = state[h][1].astype(o_ref.dtype)


def _stick_breaking(proj, *, heads, q_col, k_col, v_col, heads_per_step=6):
    bsz, s, _ = proj.shape
    T = SB_TILE
    hb = heads_per_step
    wb = hb * ATT_HEAD_DIM
    assert heads % hb == 0 and q_col % wb == 0 and k_col % wb == 0 and v_col % wb == 0 and s % T == 0
    return pl.pallas_call(
        functools.partial(_sb_kernel, heads=hb),
        out_shape=jax.ShapeDtypeStruct((bsz, s, heads * ATT_HEAD_DIM), BF16),
        grid=(bsz, heads // hb, s // T),
        in_specs=[
            pl.BlockSpec((1, T, wb), lambda b, h, i: (b, i, q_col // wb + h)),
            pl.BlockSpec((1, s, wb), lambda b, h, i: (b, 0, k_col // wb + h)),
            pl.BlockSpec((1, s, wb), lambda b, h, i: (b, 0, v_col // wb + h)),
        ],
        out_specs=pl.BlockSpec((1, T, wb), lambda b, h, i: (b, i, h)),
        compiler_params=_params(("parallel", "parallel", "arbitrary"), 40),
        name="stick_breaking",
    )(proj, proj, proj)


def _head_rms(v, g):
    ms = jnp.mean(v * v, axis=-1, keepdims=True)
    return v * lax.rsqrt(ms + 1e-6) * g


def _moba_kernel(q_ref, k_ref, v_ref, qg_ref, kg_ref, o_ref, kn_ref, kmh_ref, kml_ref, *, scale, n_blk):
    BLK = MOBA_BLOCK
    i = pl.program_id(2)

    @pl.when(i == 0)
    def _():
        kmh_ref[...] = jnp.zeros_like(kmh_ref)
        kml_ref[...] = jnp.zeros_like(kml_ref)
        for n in range(n_blk):
            rows = slice(n * BLK, (n + 1) * BLK)
            kn = _head_rms(k_ref[0, rows, :].astype(F32), kg_ref[...])
            kn_ref[rows, :] = kn.astype(BF16)
            hi, lo = _split_bf16(jnp.mean(kn, axis=0, keepdims=True))
            kmh_ref[n:n + 1, :] = hi
            kml_ref[n:n + 1, :] = lo

    qn = _head_rms(q_ref[0].astype(F32), qg_ref[...])
    q16 = (qn * scale).astype(BF16)
    nb_pad = -(-n_blk // 8) * 8
    blk = lax.broadcasted_iota(jnp.int32, (nb_pad, BLK), 0)
    row = lax.broadcasted_iota(jnp.int32, (BLK, BLK), 0)
    col = lax.broadcasted_iota(jnp.int32, (BLK, BLK), 1)

    def q_block(iv):
        rows = lambda n: slice(n * BLK, (n + 1) * BLK)
        scores = [jnp.where(col <= row, _dot_nt(q16, kn_ref[rows(iv), :]), -jnp.inf)]
        if iv > MOBA_TOPK:
            qh, ql = _split_bf16(qn)
            kmh = kmh_ref[...]
            gate = (_dot_nt(kmh, qh) + _dot_nt(kmh, ql) + _dot_nt(kml_ref[...], qh))[0:nb_pad]
            gate = jnp.where(blk < iv, gate, -jnp.inf)
            bias = jnp.zeros((nb_pad, BLK), F32)
            for n in range(iv):
                gn = gate[n:n + 1, :]
                ahead = (gate > gn) | ((gate == gn) & (blk < n))
                rank = jnp.sum(jnp.where(ahead, 1.0, 0.0), axis=0, keepdims=True)
                bias = jnp.where((blk == n) & (rank >= MOBA_TOPK), -jnp.inf, bias)
            bias = jnp.concatenate([bias, jnp.zeros((LANES - nb_pad, BLK), F32)], axis=0).T
        for n in range(iv):
            s = _dot_nt(q16, kn_ref[rows(n), :])
            if iv > MOBA_TOPK:
                s = s + bias[:, n:n + 1]
            scores.append(s)
        m = functools.reduce(jnp.maximum, [jnp.max(s, axis=-1, keepdims=True) for s in scores])
        l = 0.0
        acc = 0.0
        for s, n in zip(scores, [iv] + list(range(iv))):
            p = jnp.exp(s - m)
            l = l + jnp.sum(p, axis=-1, keepdims=True)
            acc = acc + _dot(p.astype(BF16), v_ref[0, rows(n), :])
        o_ref[0] = (acc / l).astype(o_ref.dtype)

    for iv in range(n_blk):
        pl.when(i == iv)(functools.partial(q_block, iv))


def _moba(proj, q_norm_g, k_norm_g, *, heads, q_col, k_col, v_col):
    bsz, s, _ = proj.shape
    dh = ATT_HEAD_DIM
    BLK = MOBA_BLOCK
    n_blk = s // BLK
    assert s % BLK == 0 and n_blk <= LANES
    return pl.pallas_call(
        functools.partial(_moba_kernel, scale=dh ** -0.5, n_blk=n_blk),
        out_shape=jax.ShapeDtypeStruct((bsz, s, heads * dh), BF16),
        grid=(bsz, heads, n_blk),
        in_specs=[
            pl.BlockSpec((1, BLK, dh), lambda b, h, i: (b, i, q_col // dh + h)),
            pl.BlockSpec((1, s, dh), lambda b, h, i: (b, 0, k_col // dh + h)),
            pl.BlockSpec((1, s, dh), lambda b, h, i: (b, 0, v_col // dh + h)),
            pl.BlockSpec((1, dh), lambda b, h, i: (0, 0)),
            pl.BlockSpec((1, dh), lambda b, h, i: (0, 0)),
        ],
        out_specs=pl.BlockSpec((1, BLK, dh), lambda b, h, i: (b, i, h)),
        scratch_shapes=[
            pltpu.VMEM((s, dh), BF16),
            pltpu.VMEM((LANES, dh), BF16),
            pltpu.VMEM((LANES, dh), BF16),
        ],
        compiler_params=_params(("parallel", "parallel", "arbitrary"), 32),
        name="moba",
    )(proj, proj, proj, q_norm_g.reshape(1, dh), k_norm_g.reshape(1, dh))


def kernel(x, c, norm_mix_g, norm_ffn_g, ada_w, ada_b, ffn_w_gate, ffn_w_up, ffn_w_down, ab_w_in, ab_w_out, gm_ln_g, gm_ln_b, gm_w_s, gm_b_s, ssd_conv_w, ssd_conv_b, ssd_dt_bias, ssd_a_log, ssd_d, ssd_norm_g, cd_w_in, cd_w_out, moba_q_norm_g, moba_k_norm_g):
    bsz, s, d = x.shape
    depth = ada_w.shape[0]
    mod = _ada_mod(c, ada_w, ada_b)

    a_width = gm_ln_g.shape[1]
    b_width = ssd_norm_g.shape[1]
    conv_ch = ssd_conv_w.shape[2]
    ssd_heads = ssd_dt_bias.shape[1]
    ab_main = 2 * a_width + b_width + conv_ch
    n_att = cd_w_out.shape[1] // ATT_HEAD_DIM
    d_heads = n_att // 4
    c_heads = n_att - d_heads
    c_width = c_heads * ATT_HEAD_DIM
    d_width = d_heads * ATT_HEAD_DIM

    wg16, wu16, wd16 = ffn_w_gate.astype(BF16), ffn_w_up.astype(BF16), ffn_w_down.astype(BF16)
    for layer in range(depth):
        sh_m, sc_m, g_m, sh_f, sc_f, g_f = [mod[layer, :, None, k * d:(k + 1) * d] for k in range(6)]
        if layer % 2 == 0:
            e = layer // 2
            w_in_t = ab_w_in[e].T
            proj, dt_raw = _in_proj(x, norm_mix_g[layer], sc_m, sh_m, w_in_t.astype(BF16), n_cols=ab_main,
                                    n_act_cols=2 * a_width, w_extra=w_in_t[ab_main:], w_transposed=True)
            y_a = _gmlp(proj, gm_ln_g[e], gm_ln_b[e], gm_w_s[e], gm_b_s[e], width=a_width)
            y_b = _ssd(proj, dt_raw, ssd_conv_w[e], ssd_conv_b[e], ssd_dt_bias[e], ssd_a_log[e], ssd_d[e],
                       ssd_norm_g[e], width=b_width, z_col=2 * a_width, xbc_col=2 * a_width + b_width)
            x = _out_proj(y_a, y_b, ab_w_out[e].astype(BF16), x, g_m)
        else:
            o = layer // 2
            col_scale = jnp.where(jnp.arange(cd_w_in.shape[2]) < c_width, ATT_HEAD_DIM ** -0.5 * LOG2E, 1.0)
            proj = _in_proj(x, norm_mix_g[layer], sc_m, sh_m, (cd_w_in[o] * col_scale).astype(BF16))
            y_c = _stick_breaking(proj, heads=c_heads, q_col=0, k_col=c_width, v_col=2 * c_width)
            y_d = _moba(proj, moba_q_norm_g[o], moba_k_norm_g[o], heads=d_heads,
                        q_col=3 * c_width, k_col=3 * c_width + d_width, v_col=3 * c_width + 2 * d_width)
            x = _out_proj(y_c, y_d, cd_w_out[o].astype(BF16), x, g_m)
        x = _ffn(x, norm_ffn_g[layer], sc_f, sh_f, g_f, wg16, wu16, wd16, layer)
    return x
```

```python
import functools

import jax
import jax.numpy as jnp
from jax import lax
from jax.experimental import pallas as pl
from jax.experimental.pallas import tpu as pltpu

F32 = jnp.float32
BF16 = jnp.bfloat16

LANES = 128
A_CHUNK = 128
A_GROUP = 128
SSD_HEAD_DIM = 64
SSD_GROUPS = 4
SSD_STATE = 128
SSD_CONV = 4
SSD_CHUNK = 128
ATT_HEAD_DIM = 128
MOBA_BLOCK = 256
MOBA_TOPK = 3
SB_TILE = 256
SB_DEAD = 160.0
NORM_ROWS = 64
LOG2E = 1.4426950408889634
MIB = 1 << 20


def _params(semantics, vmem_mib):
    return pltpu.CompilerParams(dimension_semantics=semantics, vmem_limit_bytes=vmem_mib * MIB)


def _split_bf16(v):
    hi = v.astype(BF16)
    lo = (v - hi.astype(F32)).astype(BF16)
    return hi, lo


def _split3_bf16(v):
    hi = v.astype(BF16)
    r = v - hi.astype(F32)
    mid = r.astype(BF16)
    lo = (r - mid.astype(F32)).astype(BF16)
    return hi, mid, lo


def _dot(a, b):
    return jnp.dot(a, b, preferred_element_type=F32)


def _dot_nt(a, b):
    return lax.dot_general(a, b, (((1,), (1,)), ((), ())), preferred_element_type=F32)


def _silu(v):
    return v * (0.5 * jnp.tanh(0.5 * v) + 0.5)


def _side_cast_plan(arrays, layer, grid):
    n_steps = 1
    for g in grid:
        n_steps *= g

    def linear_step(*ids):
        t = ids[0]
        for g, i in zip(grid[1:], ids[1:]):
            t = t * g + i
        return t

    in_specs, out_specs, out_shapes = [], [], []
    for a in arrays:
        _, rows, cols = a.shape
        rb = next(r for r in range(16, rows + 1, 16) if rows % r == 0 and rows // r <= n_steps)
        last = rows // rb - 1
        in_specs.append(pl.BlockSpec((1, rb, cols), lambda *ids, last=last: (layer, jnp.minimum(linear_step(*ids), last), 0)))
        out_specs.append(pl.BlockSpec((1, rb, cols), lambda *ids, last=last: (0, jnp.minimum(linear_step(*ids), last), 0)))
        out_shapes.append(jax.ShapeDtypeStruct((1, rows, cols), BF16))
    return in_specs, out_specs, out_shapes


def _side_cast(in_refs, out_refs):
    for i_ref, o_ref in zip(in_refs, out_refs):
        o_ref[...] = i_ref[...].astype(BF16)


def _ada_kernel(c_ref, w_ref, b_ref, o_ref):
    ca = _silu(c_ref[...])
    o_ref[0] = _dot(ca.astype(BF16), w_ref[0].astype(BF16)) + b_ref[0]


def _ada_mod(c, ada_w, ada_b):
    depth, d, n = ada_w.shape
    bsz = c.shape[0]
    tn = 1024
    return pl.pallas_call(
        _ada_kernel,
        out_shape=jax.ShapeDtypeStruct((depth, bsz, n), F32),
        grid=(depth, n // tn),
        in_specs=[
            pl.BlockSpec((bsz, d), lambda l, j: (0, 0)),
            pl.BlockSpec((1, d, tn), lambda l, j: (l, 0, j)),
            pl.BlockSpec((1, 1, tn), lambda l, j: (l, 0, j)),
        ],
        out_specs=pl.BlockSpec((1, bsz, tn), lambda l, j: (l, 0, j)),
        compiler_params=_params(("arbitrary", "arbitrary"), 40),
        name="ada_mod",
    )(c, ada_w, ada_b.reshape(depth, 1, n))


def _norm_mod_store(x_ref, g_ref, sc_ref, sh_ref, h_ref):
    scale = g_ref[...] * (1.0 + sc_ref[0])
    shift = sh_ref[0]

    def body(r, carry):
        rows = pl.ds(pl.multiple_of(r * NORM_ROWS, NORM_ROWS), NORM_ROWS)
        x = x_ref[0, rows, :]
        rs = lax.rsqrt(jnp.mean(x * x, axis=-1, keepdims=True) + 1e-6)
        h_ref[rows, :] = (x * rs * scale + shift).astype(BF16)
        return carry

    lax.fori_loop(0, h_ref.shape[0] // NORM_ROWS, body, 0)


def _inproj_kernel(*refs, n_act_tiles, has_extra, w_transposed):
    if has_extra:
        x_ref, g_ref, sc_ref, sh_ref, w_ref, wx_ref, o_ref, ox_ref, hm_ref = refs
    else:
        x_ref, g_ref, sc_ref, sh_ref, w_ref, o_ref, hm_ref = refs
    j = pl.program_id(2)
    mm = _dot_nt if w_transposed else _dot

    @pl.when(j == 0)
    def _():
        _norm_mod_store(x_ref, g_ref, sc_ref, sh_ref, hm_ref)
        if has_extra:
            ox_ref[0] = mm(hm_ref[...], wx_ref[...])

    def plain():
        o_ref[0] = mm(hm_ref[...], w_ref[...]).astype(o_ref.dtype)

    def activated():
        o_ref[0] = jax.nn.gelu(mm(hm_ref[...], w_ref[...])).astype(o_ref.dtype)

    if n_act_tiles:
        pl.when(j < n_act_tiles)(activated)
        pl.when(j >= n_act_tiles)(plain)
    else:
        plain()


def _in_proj(x, g, sc, sh, w, *, n_cols=None, n_act_cols=0, w_extra=None, w_transposed=False, tm=1024, tn=1024):
    bsz, s, d = x.shape
    n = (w.shape[0] if w_transposed else w.shape[1]) if n_cols is None else n_cols
    tm = min(tm, s)
    assert n % tn == 0 and n_act_cols % tn == 0 and tm % NORM_ROWS == 0
    has_extra = w_extra is not None
    in_specs = [
        pl.BlockSpec((1, tm, d), lambda b, i, j: (b, i, 0)),
        pl.BlockSpec((1, d), lambda b, i, j: (0, 0)),
        pl.BlockSpec((1, 1, d), lambda b, i, j: (b, 0, 0)),
        pl.BlockSpec((1, 1, d), lambda b, i, j: (b, 0, 0)),
        pl.BlockSpec((tn, d), lambda b, i, j: (j, 0)) if w_transposed else pl.BlockSpec((d, tn), lambda b, i, j: (0, j)),
    ]
    args = [x, g.reshape(1, d), sc, sh, w]
    out_shape = [jax.ShapeDtypeStruct((bsz, s, n), BF16)]
    out_specs = [pl.BlockSpec((1, tm, tn), lambda b, i, j: (b, i, j))]
    if has_extra:
        axis = 0 if w_transposed else 1
        ne = w_extra.shape[axis]
        assert 2 * ne <= LANES
        pad = [(0, 0), (0, 0)]
        pad[axis] = (0, LANES - 2 * ne)
        wx = jnp.pad(jnp.concatenate(_split_bf16(w_extra), axis=axis), pad)
        in_specs.append(pl.BlockSpec(wx.shape, lambda b, i, j: (0, 0)))
        args.append(wx)
        out_shape.append(jax.ShapeDtypeStruct((bsz, s, LANES), F32))
        out_specs.append(pl.BlockSpec((1, tm, LANES), lambda b, i, j: (b, i, 0)))
    res = pl.pallas_call(
        functools.partial(_inproj_kernel, n_act_tiles=n_act_cols // tn, has_extra=has_extra,
                          w_transposed=w_transposed),
        out_shape=out_shape,
        grid=(bsz, s // tm, n // tn),
        in_specs=in_specs,
        out_specs=out_specs,
        scratch_shapes=[pltpu.VMEM((tm, d), BF16)],
        compiler_params=_params(("parallel", "parallel", "arbitrary"), 48),
        name="in_proj",
    )(*args)
    return res if has_extra else res[0]


def _outproj_kernel(ya_ref, yb_ref, wa_ref, wb_ref, x_ref, gate_ref, o_ref):
    acc = _dot(ya_ref[0], wa_ref[...]) + _dot(yb_ref[0], wb_ref[...])
    o_ref[0] = x_ref[0] + (1.0 + gate_ref[0]) * acc


def _out_proj(ya, yb, w, x, gate, *, tm=1024, tn=512):
    bsz, s, d = x.shape
    ka, kb = ya.shape[2], yb.shape[2]
    tm = min(tm, s)
    assert w.shape[0] == ka + kb and ka % kb == 0
    return pl.pallas_call(
        _outproj_kernel,
        out_shape=jax.ShapeDtypeStruct((bsz, s, d), F32),
        grid=(bsz, s // tm, d // tn),
        in_specs=[
            pl.BlockSpec((1, tm, ka), lambda b, i, j: (b, i, 0)),
            pl.BlockSpec((1, tm, kb), lambda b, i, j: (b, i, 0)),
            pl.BlockSpec((ka, tn), lambda b, i, j: (0, j)),
            pl.BlockSpec((kb, tn), lambda b, i, j: (ka // kb, j)),
            pl.BlockSpec((1, tm, tn), lambda b, i, j: (b, i, j)),
            pl.BlockSpec((1, 1, tn), lambda b, i, j: (b, 0, j)),
        ],
        out_specs=pl.BlockSpec((1, tm, tn), lambda b, i, j: (b, i, j)),
        compiler_params=_params(("parallel", "parallel", "arbitrary"), 48),
        name="out_proj",
    )(ya, yb, w, w, x, gate)


def _ffn_kernel(x_ref, g_ref, sc_ref, sh_ref, gate_ref, wg_ref, wu_ref, wd_ref, o_ref, hf_ref):
    f = pl.program_id(2)

    @pl.when(f == 0)
    def _():
        _norm_mod_store(x_ref, g_ref, sc_ref, sh_ref, hf_ref)
        o_ref[0] = jnp.zeros(o_ref.shape[1:], F32)

    hf = hf_ref[...]
    a = _silu(_dot(hf, wg_ref[0])) * _dot(hf, wu_ref[0])
    o_ref[0] += _dot(a.astype(BF16), wd_ref[0])

    @pl.when(f == pl.num_programs(2) - 1)
    def _():
        o_ref[0] = x_ref[0] + (1.0 + gate_ref[0]) * o_ref[0]


def _ffn(x, g, sc, sh, gate, wg, wu, wd, layer, *, tm=512, tf=512):
    bsz, s, d = x.shape
    dff = wg.shape[2]
    tm = min(tm, s)
    assert dff % tf == 0 and tm % NORM_ROWS == 0
    vec = pl.BlockSpec((1, 1, d), lambda b, i, f: (b, 0, 0))
    return pl.pallas_call(
        _ffn_kernel,
        out_shape=jax.ShapeDtypeStruct((bsz, s, d), F32),
        grid=(bsz, s // tm, dff // tf),
        in_specs=[
            pl.BlockSpec((1, tm, d), lambda b, i, f: (b, i, 0)),
            pl.BlockSpec((1, d), lambda b, i, f: (0, 0)),
            vec, vec, vec,
            pl.BlockSpec((1, d, tf), lambda b, i, f: (layer, 0, f)),
            pl.BlockSpec((1, d, tf), lambda b, i, f: (layer, 0, f)),
            pl.BlockSpec((1, tf, d), lambda b, i, f: (layer, f, 0)),
        ],
        out_specs=pl.BlockSpec((1, tm, d), lambda b, i, f: (b, i, 0)),
        scratch_shapes=[pltpu.VMEM((tm, d), BF16)],
        compiler_params=_params(("parallel", "parallel", "arbitrary"), 48),
        name="ffn",
    )(x, g.reshape(1, d), sc, sh, gate, wg, wu, wd)


def _gmlp_kernel(u_ref, v_ref, lng_ref, lnb_ref, ws_ref, bst_ref, o_ref, vln_ref):
    ts, width = v_ref.shape[1], v_ref.shape[2]
    v = v_ref[0].astype(F32)
    mu = jnp.mean(v, axis=-1, keepdims=True)
    vc = v - mu
    var = jnp.mean(vc * vc, axis=-1, keepdims=True)
    vln_ref[...] = (vc * lax.rsqrt(var + 1e-5) * lng_ref[...] + lnb_ref[...]).astype(BF16)
    row = lax.broadcasted_iota(jnp.int32, (A_CHUNK, A_CHUNK), 0)
    col = lax.broadcasted_iota(jnp.int32, (A_CHUNK, A_CHUNK), 1)
    causal = col <= row
    for h in range(width // A_GROUP):
        w = jnp.where(causal, ws_ref[h], 0.0).astype(BF16)
        bias = bst_ref[:, h:h + 1]
        cols = slice(h * A_GROUP, (h + 1) * A_GROUP)
        for c in range(ts // A_CHUNK):
            rows = slice(c * A_CHUNK, (c + 1) * A_CHUNK)
            mixed = _dot(w, vln_ref[rows, cols]) + bias
            o_ref[0, rows, cols] = (u_ref[0, rows, cols].astype(F32) * mixed).astype(o_ref.dtype)


def _gmlp(proj, ln_g, ln_b, w_s, b_s, *, width, ts=512):
    bsz, s, _ = proj.shape
    heads = w_s.shape[0]
    ts = min(ts, s)
    return pl.pallas_call(
        _gmlp_kernel,
        out_shape=jax.ShapeDtypeStruct((bsz, s, width), BF16),
        grid=(bsz, s // ts),
        in_specs=[
            pl.BlockSpec((1, ts, width), lambda b, i: (b, i, 0)),
            pl.BlockSpec((1, ts, width), lambda b, i: (b, i, 1)),
            pl.BlockSpec((1, width), lambda b, i: (0, 0)),
            pl.BlockSpec((1, width), lambda b, i: (0, 0)),
            pl.BlockSpec((heads, A_CHUNK, A_CHUNK), lambda b, i: (0, 0, 0)),
            pl.BlockSpec((A_CHUNK, heads), lambda b, i: (0, 0)),
        ],
        out_specs=pl.BlockSpec((1, ts, width), lambda b, i: (b, i, 0)),
        scratch_shapes=[pltpu.VMEM((ts, width), BF16)],
        compiler_params=_params(("parallel", "parallel"), 40),
        name="gmlp",
    )(proj, proj, ln_g.reshape(1, width), ln_b.reshape(1, width), w_s, b_s.T)


def _pair_bcast(mat, q, low_half):
    rows = mat.shape[0]
    a = jnp.broadcast_to(mat[:, 2 * q:2 * q + 1], (rows, LANES))
    b = jnp.broadcast_to(mat[:, 2 * q + 1:2 * q + 2], (rows, LANES))
    return jnp.where(low_half, a, b)


def _ssd_kernel(*refs, width, n_bc, n_cast):
    z_ref, xbc_ref, dt_ref, cw_ref, cb_ref, dtb_ref, alog_ref, dskip_ref, ng_ref = refs[:9]
    o_ref = refs[9 + n_cast]
    xpad_ref, xs_ref, state_ref, y_ref = refs[10 + 2 * n_cast:]
    _side_cast(refs[9:9 + n_cast], refs[10 + n_cast:10 + 2 * n_cast])
    L = SSD_CHUNK
    c = pl.program_id(1)
    pad = 8

    @pl.when(c == 0)
    def _():
        xpad_ref[0:pad, :] = jnp.zeros((pad, xpad_ref.shape[1]), F32)
        state_ref[...] = jnp.zeros_like(state_ref)

    @pl.when(c > 0)
    def _():
        xpad_ref[0:pad, :] = xpad_ref[L:L + pad, :]

    xpad_ref[pad:pad + L, :] = xbc_ref[0].astype(F32)

    conv = cb_ref[...] + cw_ref[0:1, :] * xpad_ref[pad - 3:pad - 3 + L, :]
    for k in range(1, SSD_CONV):
        conv = conv + cw_ref[k:k + 1, :] * xpad_ref[pad - 3 + k:pad - 3 + k + L, :]
    xs_ref[...] = _silu(conv)

    heads = width // SSD_HEAD_DIM
    dt_raw = dt_ref[0]
    dt_raw = dt_raw + pltpu.roll(dt_raw, LANES - heads, axis=1)
    dt = jax.nn.softplus(dt_raw + dtb_ref[...])
    a_dt = dt * (-jnp.exp(alog_ref[...]))
    row = lax.broadcasted_iota(jnp.int32, (L, L), 0)
    col = lax.broadcasted_iota(jnp.int32, (L, L), 1)
    causal = col <= row
    tri = jnp.where(causal, 1.0, 0.0).astype(BF16)
    p0, p1, p2 = _split3_bf16(a_dt)
    a_cum = _dot(tri, p0) + _dot(tri, p1) + _dot(tri, p2)
    a_cum_t = a_cum.T
    a_last = a_cum[L - 1:L, :]
    dt_t = dt.T
    e_cum = jnp.exp(a_cum)
    dt_end = dt * jnp.exp(a_last - a_cum)

    low_half = lax.broadcasted_iota(jnp.int32, (L, LANES), 1) < SSD_HEAD_DIM
    heads_per_group = width // SSD_HEAD_DIM // SSD_GROUPS
    pairs_per_group = heads_per_group // 2
    gw = width // SSD_GROUPS
    for g in range(SSD_GROUPS):
        bm = xs_ref[:, width + g * SSD_STATE:width + (g + 1) * SSD_STATE]
        cm = xs_ref[:, width + n_bc + g * SSD_STATE:width + n_bc + (g + 1) * SSD_STATE]
        bm16 = bm.astype(BF16)
        cm16 = cm.astype(BF16)
        cb = _dot_nt(cm16, bm16)
        y_off_g = _dot(cm16, state_ref[:, g * gw:(g + 1) * gw].astype(BF16))
        bm_t16 = bm.T.astype(BF16)
        for pq in range(pairs_per_group):
            q = g * pairs_per_group + pq
            lanes = slice(q * LANES, (q + 1) * LANES)
            xs_p = xs_ref[:, lanes]
            ms = []
            for r in (2 * q, 2 * q + 1):
                seg = a_cum[:, r:r + 1] - a_cum_t[r:r + 1, :]
                decay = jnp.exp(jnp.where(causal, seg, -jnp.inf))
                ms.append((cb * decay * dt_t[r:r + 1, :]).astype(BF16))
            xs16 = xs_p.astype(BF16)
            zero = jnp.zeros_like(xs16)
            rhs = jnp.concatenate([jnp.where(low_half, xs16, zero), jnp.where(low_half, zero, xs16)], axis=0)
            y_diag = _dot(jnp.concatenate(ms, axis=1), rhs)
            e_p = _pair_bcast(e_cum, q, low_half)
            y_p = y_diag + y_off_g[:, pq * LANES:(pq + 1) * LANES] * e_p + dskip_ref[:, lanes] * xs_p
            y_ref[:, lanes] = y_p
            xdec16 = (xs_p * _pair_bcast(dt_end, q, low_half)).astype(BF16)
            s_new = _dot(bm_t16, xdec16)
            state_ref[:, lanes] = state_ref[:, lanes] * e_p[L - 1:L, :] + s_new

    z = z_ref[0].astype(F32)
    yz = y_ref[...] * _silu(z)
    ms = jnp.mean(yz * yz, axis=-1, keepdims=True)
    o_ref[0] = (yz * lax.rsqrt(ms + 1e-6) * ng_ref[...]).astype(o_ref.dtype)


def _ssd(proj, dt_raw, conv_w, conv_b, dt_bias, a_log, d_skip, norm_g, *, width, z_col, xbc_col, cast=(), cast_layer=0):
    bsz, s, _ = proj.shape
    L = SSD_CHUNK
    conv_ch = conv_w.shape[1]
    n_bc = SSD_GROUPS * SSD_STATE
    heads = width // SSD_HEAD_DIM
    assert conv_ch == width + 2 * n_bc and z_col % width == 0 and xbc_col % conv_ch == 0

    def padl(v):
        return jnp.pad(v.astype(F32), (0, LANES - heads)).reshape(1, LANES)

    grid = (bsz, s // L)
    cast_in, cast_out, cast_shapes = _side_cast_plan(cast, cast_layer, grid)
    const = lambda shape: pl.BlockSpec(shape, lambda b, c: (0,) * len(shape))
    return pl.pallas_call(
        functools.partial(_ssd_kernel, width=width, n_bc=n_bc, n_cast=len(cast)),
        out_shape=[jax.ShapeDtypeStruct((bsz, s, width), BF16)] + cast_shapes,
        grid=grid,
        in_specs=[
            pl.BlockSpec((1, L, width), lambda b, c: (b, c, z_col // width)),
            pl.BlockSpec((1, L, conv_ch), lambda b, c: (b, c, xbc_col // conv_ch)),
            pl.BlockSpec((1, L, LANES), lambda b, c: (b, c, 0)),
            const((SSD_CONV, conv_ch)),
            const((1, conv_ch)),
            const((1, LANES)),
            const((1, LANES)),
            const((1, width)),
            const((1, width)),
        ] + cast_in,
        out_specs=[pl.BlockSpec((1, L, width), lambda b, c: (b, c, 0))] + cast_out,
        scratch_shapes=[
            pltpu.VMEM((L + 8, conv_ch), F32),
            pltpu.VMEM((L, conv_ch), F32),
            pltpu.VMEM((SSD_STATE, width), F32),
            pltpu.VMEM((L, width), F32),
        ],
        compiler_params=_params(("parallel", "arbitrary"), 40),
        name="ssd",
    )(proj, proj, dt_raw, conv_w, conv_b.reshape(1, conv_ch), padl(dt_bias), padl(a_log),
      jnp.repeat(d_skip.astype(F32), SSD_HEAD_DIM).reshape(1, width), norm_g.reshape(1, width), *cast)


def _sb_kernel(*refs, heads, n_cast):
    q_ref, k_ref, v_ref = refs[:3]
    o_ref = refs[3 + n_cast]
    _side_cast(refs[3:3 + n_cast], refs[4 + n_cast:])
    _sb_tile(q_ref, k_ref, v_ref, o_ref, heads=heads)


def _sb_tile(q_ref, k_ref, v_ref, o_ref, *, heads):
    T = SB_TILE
    dh = ATT_HEAD_DIM
    i = pl.program_id(2)
    row = lax.broadcasted_iota(jnp.int32, (T, T), 0)
    col = lax.broadcasted_iota(jnp.int32, (T, T), 1)
    past = col < row
    suffix = jnp.where(row >= col, 1.0, 0.0).astype(BF16)

    def step(j, state, diagonal):
        start = pl.multiple_of(j * T, T)
        cols = [slice(h * dh, (h + 1) * dh) for h in range(heads)]
        logits = [_dot_nt(q_ref[0, :, c], k_ref[0, pl.ds(start, T), c]) for c in cols]
        cost = [jnp.maximum(x, 0.0) + jnp.log2(1.0 + jnp.exp2(-jnp.abs(x))) for x in logits]
        if diagonal:
            cost = [jnp.where(past, c, 0.0) for c in cost]
        rinc = [_dot(c.astype(BF16), suffix) for c in cost]
        w = [jnp.exp2(jnp.minimum(x - r, 0.0) - st[0]) for x, r, st in zip(logits, rinc, state)]
        if diagonal:
            w = [jnp.where(past, wh, 0.0) for wh in w]
        acc = [st[1] + _dot(wh.astype(BF16), v_ref[0, pl.ds(start, T), c]) for wh, st, c in zip(w, state, cols)]
        return tuple((st[0] + r[:, 0:1], a) for st, r, a in zip(state, rinc, acc))

    zero = (jnp.zeros((T, 1), F32), jnp.zeros((T, dh), F32))
    state = step(i, (zero,) * heads, True)

    def body(loop):
        n, _, st = loop
        st = step(i - 1 - n, st, False)
        lowest = functools.reduce(jnp.minimum, [s[0] for s in st])
        return n + 1, jnp.min(lowest), st

    def live(loop):
        n, lowest, _ = loop
        return jnp.logical_and(n < i, lowest < SB_DEAD)

    _, _, state = lax.while_loop(live, body, (jnp.int32(0), jnp.float32(0.0), state))
    for h in range(heads):
        o_ref[0, :, h * dh:(h + 1) * dh] = state[h][1].astype(o_ref.dtype)


def _stick_breaking(proj, *, heads, q_col, k_col, v_col, heads_per_step=6, cast=(), cast_layer=0):
    bsz, s, _ = proj.shape
    T = SB_TILE
    hb = heads_per_step
    wb = hb * ATT_HEAD_DIM
    assert heads % hb == 0 and q_col % wb == 0 and k_col % wb == 0 and v_col % wb == 0 and s % T == 0
    grid = (bsz, heads // hb, s // T)
    cast_in, cast_out, cast_shapes = _side_cast_plan(cast, cast_layer, grid)
    return pl.pallas_call(
        functools.partial(_sb_kernel, heads=hb, n_cast=len(cast)),
        out_shape=[jax.ShapeDtypeStruct((bsz, s, heads * ATT_HEAD_DIM), BF16)] + cast_shapes,
        grid=grid,
        in_specs=[
            pl.BlockSpec((1, T, wb), lambda b, h, i: (b, i, q_col // wb + h)),
            pl.BlockSpec((1, s, wb), lambda b, h, i: (b, 0, k_col // wb + h)),
            pl.BlockSpec((1, s, wb), lambda b, h, i: (b, 0, v_col // wb + h)),
        ] + cast_in,
        out_specs=[pl.BlockSpec((1, T, wb), lambda b, h, i: (b, i, h))] + cast_out,
        compiler_params=_params(("arbitrary", "arbitrary", "arbitrary"), 40),
        name="stick_breaking",
    )(proj, proj, proj, *cast)


def _head_rms(v, g):
    ms = jnp.mean(v * v, axis=-1, keepdims=True)
    return v * lax.rsqrt(ms + 1e-6) * g


def _moba_kernel(q_ref, k_ref, v_ref, qg_ref, kg_ref, o_ref, kn_ref, kmh_ref, kml_ref, *, scale, n_blk):
    BLK = MOBA_BLOCK
    i = pl.program_id(2)

    @pl.when(i == 0)
    def _():
        kmh_ref[...] = jnp.zeros_like(kmh_ref)
        kml_ref[...] = jnp.zeros_like(kml_ref)
        for n in range(n_blk):
            rows = slice(n * BLK, (n + 1) * BLK)
            kn = _head_rms(k_ref[0, rows, :].astype(F32), kg_ref[...])
            kn_ref[rows, :] = kn.astype(BF16)
            hi, lo = _split_bf16(jnp.mean(kn, axis=0, keepdims=True))
            kmh_ref[n:n + 1, :] = hi
            kml_ref[n:n + 1, :] = lo

    qn = _head_rms(q_ref[0].astype(F32), qg_ref[...])
    q16 = (qn * scale).astype(BF16)
    nb_pad = -(-n_blk // 8) * 8
    blk = lax.broadcasted_iota(jnp.int32, (nb_pad, BLK), 0)
    row = lax.broadcasted_iota(jnp.int32, (BLK, BLK), 0)
    col = lax.broadcasted_iota(jnp.int32, (BLK, BLK), 1)

    def q_block(iv):
        rows = lambda n: slice(n * BLK, (n + 1) * BLK)
        scores = [jnp.where(col <= row, _dot_nt(q16, kn_ref[rows(iv), :]), -jnp.inf)]
        if iv > MOBA_TOPK:
            qh, ql = _split_bf16(qn)
            kmh = kmh_ref[...]
            gate = (_dot_nt(kmh, qh) + _dot_nt(kmh, ql) + _dot_nt(kml_ref[...], qh))[0:nb_pad]
            gate = jnp.where(blk < iv, gate, -jnp.inf)
            bias = jnp.zeros((nb_pad, BLK), F32)
            for n in range(iv):
                gn = gate[n:n + 1, :]
                ahead = (gate > gn) | ((gate == gn) & (blk < n))
                rank = jnp.sum(jnp.where(ahead, 1.0, 0.0), axis=0, keepdims=True)
                bias = jnp.where((blk == n) & (rank >= MOBA_TOPK), -jnp.inf, bias)
            bias = jnp.concatenate([bias, jnp.zeros((LANES - nb_pad, BLK), F32)], axis=0).T
        for n in range(iv):
            s = _dot_nt(q16, kn_ref[rows(n), :])
            if iv > MOBA_TOPK:
                s = s + bias[:, n:n + 1]
            scores.append(s)
        m = functools.reduce(jnp.maximum, [jnp.max(s, axis=-1, keepdims=True) for s in scores])
        l = 0.0
        acc = 0.0
        for s, n in zip(scores, [iv] + list(range(iv))):
            p = jnp.exp(s - m)
            l = l + jnp.sum(p, axis=-1, keepdims=True)
            acc = acc + _dot(p.astype(BF16), v_ref[0, rows(n), :])
        o_ref[0] = (acc / l).astype(o_ref.dtype)

    for iv in range(n_blk):
        pl.when(i == iv)(functools.partial(q_block, iv))


def _moba(proj, q_norm_g, k_norm_g, *, heads, q_col, k_col, v_col):
    bsz, s, _ = proj.shape
    dh = ATT_HEAD_DIM
    BLK = MOBA_BLOCK
    n_blk = s // BLK
    assert s % BLK == 0 and n_blk <= LANES
    return pl.pallas_call(
        functools.partial(_moba_kernel, scale=dh ** -0.5, n_blk=n_blk),
        out_shape=jax.ShapeDtypeStruct((bsz, s, heads * dh), BF16),
        grid=(bsz, heads, n_blk),
        in_specs=[
            pl.BlockSpec((1, BLK, dh), lambda b, h, i: (b, i, q_col // dh + h)),
            pl.BlockSpec((1, s, dh), lambda b, h, i: (b, 0, k_col // dh + h)),
            pl.BlockSpec((1, s, dh), lambda b, h, i: (b, 0, v_col // dh + h)),
            pl.BlockSpec((1, dh), lambda b, h, i: (0, 0)),
            pl.BlockSpec((1, dh), lambda b, h, i: (0, 0)),
        ],
        out_specs=pl.BlockSpec((1, BLK, dh), lambda b, h, i: (b, i, h)),
        scratch_shapes=[
            pltpu.VMEM((s, dh), BF16),
            pltpu.VMEM((LANES, dh), BF16),
            pltpu.VMEM((LANES, dh), BF16),
        ],
        compiler_params=_params(("parallel", "parallel", "arbitrary"), 32),
        name="moba",
    )(proj, proj, proj, q_norm_g.reshape(1, dh), k_norm_g.reshape(1, dh))


def kernel(x, c, norm_mix_g, norm_ffn_g, ada_w, ada_b, ffn_w_gate, ffn_w_up, ffn_w_down, ab_w_in, ab_w_out, gm_ln_g, gm_ln_b, gm_w_s, gm_b_s, ssd_conv_w, ssd_conv_b, ssd_dt_bias, ssd_a_log, ssd_d, ssd_norm_g, cd_w_in, cd_w_out, moba_q_norm_g, moba_k_norm_g):
    bsz, s, d = x.shape
    depth = ada_w.shape[0]
    mod = _ada_mod(c, ada_w, ada_b)

    a_width = gm_ln_g.shape[1]
    b_width = ssd_norm_g.shape[1]
    conv_ch = ssd_conv_w.shape[2]
    ssd_heads = ssd_dt_bias.shape[1]
    ab_main = 2 * a_width + b_width + conv_ch
    n_att = cd_w_out.shape[1] // ATT_HEAD_DIM
    d_heads = n_att // 4
    c_heads = n_att - d_heads
    c_width = c_heads * ATT_HEAD_DIM
    d_width = d_heads * ATT_HEAD_DIM

    ffn_w = (ffn_w_gate, ffn_w_up, ffn_w_down)
    for layer in range(depth):
        sh_m, sc_m, g_m, sh_f, sc_f, g_f = [mod[layer, :, None, k * d:(k + 1) * d] for k in range(6)]
        if layer % 2 == 0:
            e = layer // 2
            w_in_t = ab_w_in[e].T
            proj, dt_raw = _in_proj(x, norm_mix_g[layer], sc_m, sh_m, w_in_t.astype(BF16), n_cols=ab_main,
                                    n_act_cols=2 * a_width, w_extra=w_in_t[ab_main:], w_transposed=True)
            y_a = _gmlp(proj, gm_ln_g[e], gm_ln_b[e], gm_w_s[e], gm_b_s[e], width=a_width)
            y_b, *ffn16 = _ssd(proj, dt_raw, ssd_conv_w[e], ssd_conv_b[e], ssd_dt_bias[e], ssd_a_log[e], ssd_d[e],
                               ssd_norm_g[e], width=b_width, z_col=2 * a_width, xbc_col=2 * a_width + b_width,
                               cast=ffn_w, cast_layer=layer)
            x = _out_proj(y_a, y_b, ab_w_out[e].astype(BF16), x, g_m)
        else:
            o = layer // 2
            col_scale = jnp.where(jnp.arange(cd_w_in.shape[2]) < c_width, ATT_HEAD_DIM ** -0.5 * LOG2E, 1.0)
            proj = _in_proj(x, norm_mix_g[layer], sc_m, sh_m, (cd_w_in[o] * col_scale).astype(BF16))
            y_c, *ffn16 = _stick_breaking(proj, heads=c_heads, q_col=0, k_col=c_width, v_col=2 * c_width,
                                          cast=ffn_w, cast_layer=layer)
            y_d = _moba(proj, moba_q_norm_g[o], moba_k_norm_g[o], heads=d_heads,
                        q_col=3 * c_width, k_col=3 * c_width + d_width, v_col=3 * c_width + 2 * d_width)
            x = _out_proj(y_c, y_d, cd_w_out[o].astype(BF16), x, g_m, tn=1024)
        x = _ffn(x, norm_ffn_g[layer], sc_f, sh_f, g_f, *ffn16, 0)
    return x
```

```python
import functools

import jax
import jax.numpy as jnp
from jax import lax
from jax.experimental import pallas as pl
from jax.experimental.pallas import tpu as pltpu

F32 = jnp.float32
BF16 = jnp.bfloat16

LANES = 128
A_CHUNK = 128
A_GROUP = 128
SSD_HEAD_DIM = 64
SSD_GROUPS = 4
SSD_STATE = 128
SSD_CONV = 4
SSD_CHUNK = 128
ATT_HEAD_DIM = 128
MOBA_BLOCK = 256
MOBA_TOPK = 3
SB_TILE = 256
SB_DEAD = 160.0
NORM_ROWS = 64
LOG2E = 1.4426950408889634
MIB = 1 << 20


def _params(semantics, vmem_mib):
    return pltpu.CompilerParams(dimension_semantics=semantics, vmem_limit_bytes=vmem_mib * MIB)


def _split_bf16(v):
    hi = v.astype(BF16)
    lo = (v - hi.astype(F32)).astype(BF16)
    return hi, lo


def _split3_bf16(v):
    hi = v.astype(BF16)
    r = v - hi.astype(F32)
    mid = r.astype(BF16)
    lo = (r - mid.astype(F32)).astype(BF16)
    return hi, mid, lo


def _dot(a, b):
    return jnp.dot(a, b, preferred_element_type=F32)


def _dot_nt(a, b):
    return lax.dot_general(a, b, (((1,), (1,)), ((), ())), preferred_element_type=F32)


def _silu(v):
    return v * (0.5 * jnp.tanh(0.5 * v) + 0.5)


def _side_cast_plan(arrays, layer, grid):
    n_steps = 1
    for g in grid:
        n_steps *= g

    def linear_step(*ids):
        t = ids[0]
        for g, i in zip(grid[1:], ids[1:]):
            t = t * g + i
        return t

    in_specs, out_specs, out_shapes = [], [], []
    for a in arrays:
        _, rows, cols = a.shape
        rb = next(r for r in range(16, rows + 1, 16) if rows % r == 0 and rows // r <= n_steps)
        last = rows // rb - 1
        in_specs.append(pl.BlockSpec((1, rb, cols), lambda *ids, last=last: (layer, jnp.minimum(linear_step(*ids), last), 0)))
        out_specs.append(pl.BlockSpec((1, rb, cols), lambda *ids, last=last: (0, jnp.minimum(linear_step(*ids), last), 0)))
        out_shapes.append(jax.ShapeDtypeStruct((1, rows, cols), BF16))
    return in_specs, out_specs, out_shapes


def _side_cast(in_refs, out_refs):
    for i_ref, o_ref in zip(in_refs, out_refs):
        o_ref[...] = i_ref[...].astype(BF16)


def _ada_kernel(c_ref, w_ref, b_ref, o_ref):
    ca = _silu(c_ref[...])
    o_ref[0] = _dot(ca.astype(BF16), w_ref[0].astype(BF16)) + b_ref[0]


def _ada_mod(c, ada_w, ada_b):
    depth, d, n = ada_w.shape
    bsz = c.shape[0]
    tn = 1024
    return pl.pallas_call(
        _ada_kernel,
        out_shape=jax.ShapeDtypeStruct((depth, bsz, n), F32),
        grid=(depth, n // tn),
        in_specs=[
            pl.BlockSpec((bsz, d), lambda l, j: (0, 0)),
            pl.BlockSpec((1, d, tn), lambda l, j: (l, 0, j)),
            pl.BlockSpec((1, 1, tn), lambda l, j: (l, 0, j)),
        ],
        out_specs=pl.BlockSpec((1, bsz, tn), lambda l, j: (l, 0, j)),
        compiler_params=_params(("arbitrary", "arbitrary"), 40),
        name="ada_mod",
    )(c, ada_w, ada_b.reshape(depth, 1, n))


def _norm_chunk(x_ref, scale, shift, h_ref, chunk):
    rows = pl.ds(pl.multiple_of(chunk * NORM_ROWS, NORM_ROWS), NORM_ROWS)
    x = x_ref[0, rows, :]
    rs = lax.rsqrt(jnp.mean(x * x, axis=-1, keepdims=True) + 1e-6)
    h_ref[rows, :] = (x * rs * scale + shift).astype(BF16)


def _norm_mod_store(x_ref, g_ref, sc_ref, sh_ref, h_ref):
    scale = g_ref[...] * (1.0 + sc_ref[0])
    shift = sh_ref[0]

    def body(r, carry):
        _norm_chunk(x_ref, scale, shift, h_ref, r)
        return carry

    lax.fori_loop(0, h_ref.shape[0] // NORM_ROWS, body, 0)


def _norm_ahead(x_ref, g_ref, sc_ref, sh_ref, h_ref, j, n_j):
    n_chunks = h_ref.shape[0] // NORM_ROWS
    per_step = -(-n_chunks // (n_j - 1))
    scale = g_ref[...] * (1.0 + sc_ref[0])
    shift = sh_ref[0]
    for k in range(per_step):
        _norm_chunk(x_ref, scale, shift, h_ref, jnp.clip((j - 1) * per_step + k, 0, n_chunks - 1))


def _ahead_maps(n_b, n_i):
    def tile(b, i, j):
        r = b * n_i + i
        t = jnp.where(jnp.logical_and(r == 0, j == 0), 0, jnp.minimum(r + 1, n_b * n_i - 1))
        return t // n_i, t % n_i

    def x_map(b, i, j):
        tb, ti = tile(b, i, j)
        return tb, ti, 0

    def vec_map(b, i, j):
        return tile(b, i, j)[0], 0, 0

    return x_map, vec_map


def _inproj_kernel(*refs, n_j, n_act_tiles, has_extra, w_transposed):
    if has_extra:
        x_ref, g_ref, sc_ref, sh_ref, w_ref, wx_ref, o_ref, ox_ref, hm_a, hm_b = refs
    else:
        x_ref, g_ref, sc_ref, sh_ref, w_ref, o_ref, hm_a, hm_b = refs
    j = pl.program_id(2)
    r = pl.program_id(0) * pl.num_programs(1) + pl.program_id(1)
    mm = _dot_nt if w_transposed else _dot

    @pl.when(jnp.logical_and(r == 0, j == 0))
    def _():
        _norm_mod_store(x_ref, g_ref, sc_ref, sh_ref, hm_a)

    def tile_step(cur, nxt, act):
        def run():
            if has_extra:
                @pl.when(j == 0)
                def _():
                    ox_ref[0] = mm(cur[...], wx_ref[...])
            _norm_ahead(x_ref, g_ref, sc_ref, sh_ref, nxt, j, n_j)
            acc = mm(cur[...], w_ref[...])
            o_ref[0] = (jax.nn.gelu(acc) if act else acc).astype(o_ref.dtype)
        return run

    for parity, (cur, nxt) in enumerate(((hm_a, hm_b), (hm_b, hm_a))):
        mine = r % 2 == parity
        if n_act_tiles:
            pl.when(jnp.logical_and(mine, j < n_act_tiles))(tile_step(cur, nxt, True))
            pl.when(jnp.logical_and(mine, j >= n_act_tiles))(tile_step(cur, nxt, False))
        else:
            pl.when(mine)(tile_step(cur, nxt, False))


def _in_proj(x, g, sc, sh, w, *, n_cols=None, n_act_cols=0, w_extra=None, w_transposed=False, tm=1024, tn=1024):
    bsz, s, d = x.shape
    n = (w.shape[0] if w_transposed else w.shape[1]) if n_cols is None else n_cols
    tm = min(tm, s)
    assert n % tn == 0 and n_act_cols % tn == 0 and tm % NORM_ROWS == 0 and n // tn >= 2
    has_extra = w_extra is not None
    x_map, vec_map = _ahead_maps(bsz, s // tm)
    in_specs = [
        pl.BlockSpec((1, tm, d), x_map),
        pl.BlockSpec((1, d), lambda b, i, j: (0, 0)),
        pl.BlockSpec((1, 1, d), vec_map),
        pl.BlockSpec((1, 1, d), vec_map),
        pl.BlockSpec((tn, d), lambda b, i, j: (j, 0)) if w_transposed else pl.BlockSpec((d, tn), lambda b, i, j: (0, j)),
    ]
    args = [x, g.reshape(1, d), sc, sh, w]
    out_shape = [jax.ShapeDtypeStruct((bsz, s, n), BF16)]
    out_specs = [pl.BlockSpec((1, tm, tn), lambda b, i, j: (b, i, j))]
    if has_extra:
        axis = 0 if w_transposed else 1
        ne = w_extra.shape[axis]
        assert 2 * ne <= LANES
        pad = [(0, 0), (0, 0)]
        pad[axis] = (0, LANES - 2 * ne)
        wx = jnp.pad(jnp.concatenate(_split_bf16(w_extra), axis=axis), pad)
        in_specs.append(pl.BlockSpec(wx.shape, lambda b, i, j: (0, 0)))
        args.append(wx)
        out_shape.append(jax.ShapeDtypeStruct((bsz, s, LANES), F32))
        out_specs.append(pl.BlockSpec((1, tm, LANES), lambda b, i, j: (b, i, 0)))
    res = pl.pallas_call(
        functools.partial(_inproj_kernel, n_j=n // tn, n_act_tiles=n_act_cols // tn, has_extra=has_extra,
                          w_transposed=w_transposed),
        out_shape=out_shape,
        grid=(bsz, s // tm, n // tn),
        in_specs=in_specs,
        out_specs=out_specs,
        scratch_shapes=[pltpu.VMEM((tm, d), BF16)] * 2,
        compiler_params=_params(("arbitrary", "arbitrary", "arbitrary"), 52),
        name="in_proj",
    )(*args)
    return res if has_extra else res[0]


def _outproj_kernel(ya_ref, yb_ref, wa_ref, wb_ref, x_ref, gate_ref, o_ref):
    acc = _dot(ya_ref[0], wa_ref[...]) + _dot(yb_ref[0], wb_ref[...])
    o_ref[0] = x_ref[0] + (1.0 + gate_ref[0]) * acc


def _out_proj(ya, yb, w, x, gate, *, tm=1024, tn=512):
    bsz, s, d = x.shape
    ka, kb = ya.shape[2], yb.shape[2]
    tm = min(tm, s)
    assert w.shape[0] == ka + kb and ka % kb == 0
    return pl.pallas_call(
        _outproj_kernel,
        out_shape=jax.ShapeDtypeStruct((bsz, s, d), F32),
        grid=(bsz, s // tm, d // tn),
        in_specs=[
            pl.BlockSpec((1, tm, ka), lambda b, i, j: (b, i, 0)),
            pl.BlockSpec((1, tm, kb), lambda b, i, j: (b, i, 0)),
            pl.BlockSpec((ka, tn), lambda b, i, j: (0, j)),
            pl.BlockSpec((kb, tn), lambda b, i, j: (ka // kb, j)),
            pl.BlockSpec((1, tm, tn), lambda b, i, j: (b, i, j)),
            pl.BlockSpec((1, 1, tn), lambda b, i, j: (b, 0, j)),
        ],
        out_specs=pl.BlockSpec((1, tm, tn), lambda b, i, j: (b, i, j)),
        compiler_params=_params(("parallel", "parallel", "arbitrary"), 48),
        name="out_proj",
    )(ya, yb, w, w, x, gate)


def _ffn_kernel(x_ref, xn_ref, g_ref, sc_ref, sh_ref, gate_ref, wg_ref, wu_ref, wd_ref, o_ref, hf_a, hf_b, *, n_f):
    f = pl.program_id(2)
    r = pl.program_id(0) * pl.num_programs(1) + pl.program_id(1)

    @pl.when(jnp.logical_and(r == 0, f == 0))
    def _():
        _norm_mod_store(xn_ref, g_ref, sc_ref, sh_ref, hf_a)

    @pl.when(f == 0)
    def _():
        o_ref[0] = jnp.zeros(o_ref.shape[1:], F32)

    def tile_step(cur, nxt):
        def run():
            _norm_ahead(xn_ref, g_ref, sc_ref, sh_ref, nxt, f, n_f)
            hf = cur[...]
            a = _silu(_dot(hf, wg_ref[0])) * _dot(hf, wu_ref[0])
            o_ref[0] += _dot(a.astype(BF16), wd_ref[0])
        return run

    pl.when(r % 2 == 0)(tile_step(hf_a, hf_b))
    pl.when(r % 2 == 1)(tile_step(hf_b, hf_a))

    @pl.when(f == n_f - 1)
    def _():
        o_ref[0] = x_ref[0] + (1.0 + gate_ref[0]) * o_ref[0]


def _ffn(x, g, sc, sh, gate, wg, wu, wd, layer, *, tm=512, tf=512):
    bsz, s, d = x.shape
    dff = wg.shape[2]
    tm = min(tm, s)
    assert dff % tf == 0 and tm % NORM_ROWS == 0 and dff // tf >= 2
    x_map, vec_map = _ahead_maps(bsz, s // tm)
    return pl.pallas_call(
        functools.partial(_ffn_kernel, n_f=dff // tf),
        out_shape=jax.ShapeDtypeStruct((bsz, s, d), F32),
        grid=(bsz, s // tm, dff // tf),
        in_specs=[
            pl.BlockSpec((1, tm, d), lambda b, i, f: (b, i, 0)),
            pl.BlockSpec((1, tm, d), x_map),
            pl.BlockSpec((1, d), lambda b, i, f: (0, 0)),
            pl.BlockSpec((1, 1, d), vec_map),
            pl.BlockSpec((1, 1, d), vec_map),
            pl.BlockSpec((1, 1, d), lambda b, i, f: (b, 0, 0)),
            pl.BlockSpec((1, d, tf), lambda b, i, f: (layer, 0, f)),
            pl.BlockSpec((1, d, tf), lambda b, i, f: (layer, 0, f)),
            pl.BlockSpec((1, tf, d), lambda b, i, f: (layer, f, 0)),
        ],
        out_specs=pl.BlockSpec((1, tm, d), lambda b, i, f: (b, i, 0)),
        scratch_shapes=[pltpu.VMEM((tm, d), BF16)] * 2,
        compiler_params=_params(("arbitrary", "arbitrary", "arbitrary"), 52),
        name="ffn",
    )(x, x, g.reshape(1, d), sc, sh, gate, wg, wu, wd)


def _gmlp_kernel(u_ref, v_ref, lng_ref, lnb_ref, ws_ref, bst_ref, o_ref, vln_ref):
    ts, width = v_ref.shape[1], v_ref.shape[2]
    v = v_ref[0].astype(F32)
    mu = jnp.mean(v, axis=-1, keepdims=True)
    vc = v - mu
    var = jnp.mean(vc * vc, axis=-1, keepdims=True)
    vln_ref[...] = (vc * lax.rsqrt(var + 1e-5) * lng_ref[...] + lnb_ref[...]).astype(BF16)
    row = lax.broadcasted_iota(jnp.int32, (A_CHUNK, A_CHUNK), 0)
    col = lax.broadcasted_iota(jnp.int32, (A_CHUNK, A_CHUNK), 1)
    causal = col <= row
    for h in range(width // A_GROUP):
        w = jnp.where(causal, ws_ref[h], 0.0).astype(BF16)
        bias = bst_ref[:, h:h + 1]
        cols = slice(h * A_GROUP, (h + 1) * A_GROUP)
        for c in range(ts // A_CHUNK):
            rows = slice(c * A_CHUNK, (c + 1) * A_CHUNK)
            mixed = _dot(w, vln_ref[rows, cols]) + bias
            o_ref[0, rows, cols] = (u_ref[0, rows, cols].astype(F32) * mixed).astype(o_ref.dtype)


def _gmlp(proj, ln_g, ln_b, w_s, b_s, *, width, ts=512):
    bsz, s, _ = proj.shape
    heads = w_s.shape[0]
    ts = min(ts, s)
    return pl.pallas_call(
        _gmlp_kernel,
        out_shape=jax.ShapeDtypeStruct((bsz, s, width), BF16),
        grid=(bsz, s // ts),
        in_specs=[
            pl.BlockSpec((1, ts, width), lambda b, i: (b, i, 0)),
            pl.BlockSpec((1, ts, width), lambda b, i: (b, i, 1)),
            pl.BlockSpec((1, width), lambda b, i: (0, 0)),
            pl.BlockSpec((1, width), lambda b, i: (0, 0)),
            pl.BlockSpec((heads, A_CHUNK, A_CHUNK), lambda b, i: (0, 0, 0)),
            pl.BlockSpec((A_CHUNK, heads), lambda b, i: (0, 0)),
        ],
        out_specs=pl.BlockSpec((1, ts, width), lambda b, i: (b, i, 0)),
        scratch_shapes=[pltpu.VMEM((ts, width), BF16)],
        compiler_params=_params(("parallel", "parallel"), 40),
        name="gmlp",
    )(proj, proj, ln_g.reshape(1, width), ln_b.reshape(1, width), w_s, b_s.T)


def _pair_bcast(mat, q, low_half):
    rows = mat.shape[0]
    a = jnp.broadcast_to(mat[:, 2 * q:2 * q + 1], (rows, LANES))
    b = jnp.broadcast_to(mat[:, 2 * q + 1:2 * q + 2], (rows, LANES))
    return jnp.where(low_half, a, b)


def _ssd_kernel(*refs, width, n_bc, n_cast):
    z_ref, xbc_ref, dt_ref, cw_ref, cb_ref, dtb_ref, alog_ref, dskip_ref, ng_ref = refs[:9]
    o_ref = refs[9 + n_cast]
    xpad_ref, xs_ref, state_ref, y_ref = refs[10 + 2 * n_cast:]
    _side_cast(refs[9:9 + n_cast], refs[10 + n_cast:10 + 2 * n_cast])
    L = SSD_CHUNK
    c = pl.program_id(1)
    pad = 8

    @pl.when(c == 0)
    def _():
        xpad_ref[0:pad, :] = jnp.zeros((pad, xpad_ref.shape[1]), F32)
        state_ref[...] = jnp.zeros_like(state_ref)

    @pl.when(c > 0)
    def _():
        xpad_ref[0:pad, :] = xpad_ref[L:L + pad, :]

    xpad_ref[pad:pad + L, :] = xbc_ref[0].astype(F32)

    conv = cb_ref[...] + cw_ref[0:1, :] * xpad_ref[pad - 3:pad - 3 + L, :]
    for k in range(1, SSD_CONV):
        conv = conv + cw_ref[k:k + 1, :] * xpad_ref[pad - 3 + k:pad - 3 + k + L, :]
    xs_ref[...] = _silu(conv)

    heads = width // SSD_HEAD_DIM
    dt_raw = dt_ref[0]
    dt_raw = dt_raw + pltpu.roll(dt_raw, LANES - heads, axis=1)
    dt = jax.nn.softplus(dt_raw + dtb_ref[...])
    a_dt = dt * (-jnp.exp(alog_ref[...]))
    row = lax.broadcasted_iota(jnp.int32, (L, L), 0)
    col = lax.broadcasted_iota(jnp.int32, (L, L), 1)
    causal = col <= row
    tri = jnp.where(causal, 1.0, 0.0).astype(BF16)
    p0, p1, p2 = _split3_bf16(a_dt)
    a_cum = _dot(tri, p0) + _dot(tri, p1) + _dot(tri, p2)
    a_cum_t = a_cum.T
    a_last = a_cum[L - 1:L, :]
    dt_t = dt.T
    e_cum = jnp.exp(a_cum)
    dt_end = dt * jnp.exp(a_last - a_cum)

    low_half = lax.broadcasted_iota(jnp.int32, (L, LANES), 1) < SSD_HEAD_DIM
    heads_per_group = width // SSD_HEAD_DIM // SSD_GROUPS
    pairs_per_group = heads_per_group // 2
    gw = width // SSD_GROUPS
    for g in range(SSD_GROUPS):
        bm = xs_ref[:, width + g * SSD_STATE:width + (g + 1) * SSD_STATE]
        cm = xs_ref[:, width + n_bc + g * SSD_STATE:width + n_bc + (g + 1) * SSD_STATE]
        bm16 = bm.astype(BF16)
        cm16 = cm.astype(BF16)
        cb = _dot_nt(cm16, bm16)
        y_off_g = _dot(cm16, state_ref[:, g * gw:(g + 1) * gw].astype(BF16))
        bm_t16 = bm.T.astype(BF16)
        for pq in range(pairs_per_group):
            q = g * pairs_per_group + pq
            lanes = slice(q * LANES, (q + 1) * LANES)
            xs_p = xs_ref[:, lanes]
            ms = []
            for r in (2 * q, 2 * q + 1):
                seg = a_cum[:, r:r + 1] - a_cum_t[r:r + 1, :]
                decay = jnp.exp(jnp.where(causal, seg, -jnp.inf))
                ms.append((cb * decay * dt_t[r:r + 1, :]).astype(BF16))
            xs16 = xs_p.astype(BF16)
            zero = jnp.zeros_like(xs16)
            rhs = jnp.concatenate([jnp.where(low_half, xs16, zero), jnp.where(low_half, zero, xs16)], axis=0)
            y_diag = _dot(jnp.concatenate(ms, axis=1), rhs)
            e_p = _pair_bcast(e_cum, q, low_half)
            y_p = y_diag + y_off_g[:, pq * LANES:(pq + 1) * LANES] * e_p + dskip_ref[:, lanes] * xs_p
            y_ref[:, lanes] = y_p
            xdec16 = (xs_p * _pair_bcast(dt_end, q, low_half)).astype(BF16)
            s_new = _dot(bm_t16, xdec16)
            state_ref[:, lanes] = state_ref[:, lanes] * e_p[L - 1:L, :] + s_new

    z = z_ref[0].astype(F32)
    yz = y_ref[...] * _silu(z)
    ms = jnp.mean(yz * yz, axis=-1, keepdims=True)
    o_ref[0] = (yz * lax.rsqrt(ms + 1e-6) * ng_ref[...]).astype(o_ref.dtype)


def _ssd(proj, dt_raw, conv_w, conv_b, dt_bias, a_log, d_skip, norm_g, *, width, z_col, xbc_col, cast=(), cast_layer=0):
    bsz, s, _ = proj.shape
    L = SSD_CHUNK
    conv_ch = conv_w.shape[1]
    n_bc = SSD_GROUPS * SSD_STATE
    heads = width // SSD_HEAD_DIM
    assert conv_ch == width + 2 * n_bc and z_col % width == 0 and xbc_col % conv_ch == 0

    def padl(v):
        return jnp.pad(v.astype(F32), (0, LANES - heads)).reshape(1, LANES)

    grid = (bsz, s // L)
    cast_in, cast_out, cast_shapes = _side_cast_plan(cast, cast_layer, grid)
    const = lambda shape: pl.BlockSpec(shape, lambda b, c: (0,) * len(shape))
    return pl.pallas_call(
        functools.partial(_ssd_kernel, width=width, n_bc=n_bc, n_cast=len(cast)),
        out_shape=[jax.ShapeDtypeStruct((bsz, s, width), BF16)] + cast_shapes,
        grid=grid,
        in_specs=[
            pl.BlockSpec((1, L, width), lambda b, c: (b, c, z_col // width)),
            pl.BlockSpec((1, L, conv_ch), lambda b, c: (b, c, xbc_col // conv_ch)),
            pl.BlockSpec((1, L, LANES), lambda b, c: (b, c, 0)),
            const((SSD_CONV, conv_ch)),
            const((1, conv_ch)),
            const((1, LANES)),
            const((1, LANES)),
            const((1, width)),
            const((1, width)),
        ] + cast_in,
        out_specs=[pl.BlockSpec((1, L, width), lambda b, c: (b, c, 0))] + cast_out,
        scratch_shapes=[
            pltpu.VMEM((L + 8, conv_ch), F32),
            pltpu.VMEM((L, conv_ch), F32),
            pltpu.VMEM((SSD_STATE, width), F32),
            pltpu.VMEM((L, width), F32),
        ],
        compiler_params=_params(("parallel", "arbitrary"), 40),
        name="ssd",
    )(proj, proj, dt_raw, conv_w, conv_b.reshape(1, conv_ch), padl(dt_bias), padl(a_log),
      jnp.repeat(d_skip.astype(F32), SSD_HEAD_DIM).reshape(1, width), norm_g.reshape(1, width), *cast)


def _sb_kernel(*refs, heads, n_cast):
    q_ref, k_ref, v_ref = refs[:3]
    o_ref = refs[3 + n_cast]
    _side_cast(refs[3:3 + n_cast], refs[4 + n_cast:])
    _sb_tile(q_ref, k_ref, v_ref, o_ref, heads=heads)


def _sb_tile(q_ref, k_ref, v_ref, o_ref, *, heads):
    T = SB_TILE
    dh = ATT_HEAD_DIM
    i = pl.program_id(2)
    row = lax.broadcasted_iota(jnp.int32, (T, T), 0)
    col = lax.broadcasted_iota(jnp.int32, (T, T), 1)
    past = col < row
    suffix = jnp.where(row >= col, 1.0, 0.0).astype(BF16)

    def step(j, state, diagonal):
        start = pl.multiple_of(j * T, T)
        cols = [slice(h * dh, (h + 1) * dh) for h in range(heads)]
        logits = [_dot_nt(q_ref[0, :, c], k_ref[0, pl.ds(start, T), c]) for c in cols]
        cost = [jnp.maximum(x, 0.0) + jnp.log2(1.0 + jnp.exp2(-jnp.abs(x))) for x in logits]
        if diagonal:
            cost = [jnp.where(past, c, 0.0) for c in cost]
        rinc = [_dot(c.astype(BF16), suffix) for c in cost]
        w = [jnp.exp2(jnp.minimum(x - r, 0.0) - st[0]) for x, r, st in zip(logits, rinc, state)]
        if diagonal:
            w = [jnp.where(past, wh, 0.0) for wh in w]
        acc = [st[1] + _dot(wh.astype(BF16), v_ref[0, pl.ds(start, T), c]) for wh, st, c in zip(w, state, cols)]
        return tuple((st[0] + r[:, 0:1], a) for st, r, a in zip(state, rinc, acc))

    zero = (jnp.zeros((T, 1), F32), jnp.zeros((T, dh), F32))
    state = step(i, (zero,) * heads, True)

    def body(loop):
        n, _, st = loop
        st = step(i - 1 - n, st, False)
        lowest = functools.reduce(jnp.minimum, [s[0] for s in st])
        return n + 1, jnp.min(lowest), st

    def live(loop):
        n, lowest, _ = loop
        return jnp.logical_and(n < i, lowest < SB_DEAD)

    _, _, state = lax.while_loop(live, body, (jnp.int32(0), jnp.float32(0.0), state))
    for h in range(heads):
        o_ref[0, :, h * dh:(h + 1) * dh] = state[h][1].astype(o_ref.dtype)


def _stick_breaking(proj, *, heads, q_col, k_col, v_col, heads_per_step=6, cast=(), cast_layer=0):
    bsz, s, _ = proj.shape
    T = SB_TILE
    hb = heads_per_step
    wb = hb * ATT_HEAD_DIM
    assert heads % hb == 0 and q_col % wb == 0 and k_col % wb == 0 and v_col % wb == 0 and s % T == 0
    grid = (bsz, heads // hb, s // T)
    cast_in, cast_out, cast_shapes = _side_cast_plan(cast, cast_layer, grid)
    return pl.pallas_call(
        functools.partial(_sb_kernel, heads=hb, n_cast=len(cast)),
        out_shape=[jax.ShapeDtypeStruct((bsz, s, heads * ATT_HEAD_DIM), BF16)] + cast_shapes,
        grid=grid,
        in_specs=[
            pl.BlockSpec((1, T, wb), lambda b, h, i: (b, i, q_col // wb + h)),
            pl.BlockSpec((1, s, wb), lambda b, h, i: (b, 0, k_col // wb + h)),
            pl.BlockSpec((1, s, wb), lambda b, h, i: (b, 0, v_col // wb + h)),
        ] + cast_in,
        out_specs=[pl.BlockSpec((1, T, wb), lambda b, h, i: (b, i, h))] + cast_out,
        compiler_params=_params(("arbitrary", "arbitrary", "arbitrary"), 40),
        name="stick_breaking",
    )(proj, proj, proj, *cast)


def _head_rms(v, g):
    ms = jnp.mean(v * v, axis=-1, keepdims=True)
    return v * lax.rsqrt(ms + 1e-6) * g


def _moba_kernel(q_ref, k_ref, v_ref, qg_ref, kg_ref, o_ref, kn_ref, kmh_ref, kml_ref, *, scale, n_blk):
    BLK = MOBA_BLOCK
    i = pl.program_id(2)

    @pl.when(i == 0)
    def _():
        kmh_ref[...] = jnp.zeros_like(kmh_ref)
        kml_ref[...] = jnp.zeros_like(kml_ref)
        for n in range(n_blk):
            rows = slice(n * BLK, (n + 1) * BLK)
            kn = _head_rms(k_ref[0, rows, :].astype(F32), kg_ref[...])
            kn_ref[rows, :] = kn.astype(BF16)
            hi, lo = _split_bf16(jnp.mean(kn, axis=0, keepdims=True))
            kmh_ref[n:n + 1, :] = hi
            kml_ref[n:n + 1, :] = lo

    qn = _head_rms(q_ref[0].astype(F32), qg_ref[...])
    q16 = (qn * scale).astype(BF16)
    nb_pad = -(-n_blk // 8) * 8
    blk = lax.broadcasted_iota(jnp.int32, (nb_pad, BLK), 0)
    row = lax.broadcasted_iota(jnp.int32, (BLK, BLK), 0)
    col = lax.broadcasted_iota(jnp.int32, (BLK, BLK), 1)

    def q_block(iv):
        rows = lambda n: slice(n * BLK, (n + 1) * BLK)
        scores = [jnp.where(col <= row, _dot_nt(q16, kn_ref[rows(iv), :]), -jnp.inf)]
        if iv > MOBA_TOPK:
            qh, ql = _split_bf16(qn)
            kmh = kmh_ref[...]
            gate = (_dot_nt(kmh, qh) + _dot_nt(kmh, ql) + _dot_nt(kml_ref[...], qh))[0:nb_pad]
            gate = jnp.where(blk < iv, gate, -jnp.inf)
            bias = jnp.zeros((nb_pad, BLK), F32)
            for n in range(iv):
                gn = gate[n:n + 1, :]
                ahead = (gate > gn) | ((gate == gn) & (blk < n))
                rank = jnp.sum(jnp.where(ahead, 1.0, 0.0), axis=0, keepdims=True)
                bias = jnp.where((blk == n) & (rank >= MOBA_TOPK), -jnp.inf, bias)
            bias = jnp.concatenate([bias, jnp.zeros((LANES - nb_pad, BLK), F32)], axis=0).T
        for n in range(iv):
            s = _dot_nt(q16, kn_ref[rows(n), :])
            if iv > MOBA_TOPK:
                s = s + bias[:, n:n + 1]
            scores.append(s)
        m = functools.reduce(jnp.maximum, [jnp.max(s, axis=-1, keepdims=True) for s in scores])
        l = 0.0
        acc = 0.0
        for s, n in zip(scores, [iv] + list(range(iv))):
            p = jnp.exp(s - m)
            l = l + jnp.sum(p, axis=-1, keepdims=True)
            acc = acc + _dot(p.astype(BF16), v_ref[0, rows(n), :])
        o_ref[0] = (acc / l).astype(o_ref.dtype)

    for iv in range(n_blk):
        pl.when(i == iv)(functools.partial(q_block, iv))


def _moba(proj, q_norm_g, k_norm_g, *, heads, q_col, k_col, v_col):
    bsz, s, _ = proj.shape
    dh = ATT_HEAD_DIM
    BLK = MOBA_BLOCK
    n_blk = s // BLK
    assert s % BLK == 0 and n_blk <= LANES
    return pl.pallas_call(
        functools.partial(_moba_kernel, scale=dh ** -0.5, n_blk=n_blk),
        out_shape=jax.ShapeDtypeStruct((bsz, s, heads * dh), BF16),
        grid=(bsz, heads, n_blk),
        in_specs=[
            pl.BlockSpec((1, BLK, dh), lambda b, h, i: (b, i, q_col // dh + h)),
            pl.BlockSpec((1, s, dh), lambda b, h, i: (b, 0, k_col // dh + h)),
            pl.BlockSpec((1, s, dh), lambda b, h, i: (b, 0, v_col // dh + h)),
            pl.BlockSpec((1, dh), lambda b, h, i: (0, 0)),
            pl.BlockSpec((1, dh), lambda b, h, i: (0, 0)),
        ],
        out_specs=pl.BlockSpec((1, BLK, dh), lambda b, h, i: (b, i, h)),
        scratch_shapes=[
            pltpu.VMEM((s, dh), BF16),
            pltpu.VMEM((LANES, dh), BF16),
            pltpu.VMEM((LANES, dh), BF16),
        ],
        compiler_params=_params(("parallel", "parallel", "arbitrary"), 32),
        name="moba",
    )(proj, proj, proj, q_norm_g.reshape(1, dh), k_norm_g.reshape(1, dh))


def kernel(x, c, norm_mix_g, norm_ffn_g, ada_w, ada_b, ffn_w_gate, ffn_w_up, ffn_w_down, ab_w_in, ab_w_out, gm_ln_g, gm_ln_b, gm_w_s, gm_b_s, ssd_conv_w, ssd_conv_b, ssd_dt_bias, ssd_a_log, ssd_d, ssd_norm_g, cd_w_in, cd_w_out, moba_q_norm_g, moba_k_norm_g):
    bsz, s, d = x.shape
    depth = ada_w.shape[0]
    mod = _ada_mod(c, ada_w, ada_b)

    a_width = gm_ln_g.shape[1]
    b_width = ssd_norm_g.shape[1]
    conv_ch = ssd_conv_w.shape[2]
    ssd_heads = ssd_dt_bias.shape[1]
    ab_main = 2 * a_width + b_width + conv_ch
    n_att = cd_w_out.shape[1] // ATT_HEAD_DIM
    d_heads = n_att // 4
    c_heads = n_att - d_heads
    c_width = c_heads * ATT_HEAD_DIM
    d_width = d_heads * ATT_HEAD_DIM

    ffn_w = (ffn_w_gate, ffn_w_up, ffn_w_down)
    for layer in range(depth):
        sh_m, sc_m, g_m, sh_f, sc_f, g_f = [mod[layer, :, None, k * d:(k + 1) * d] for k in range(6)]
        if layer % 2 == 0:
            e = layer // 2
            w_in_t = ab_w_in[e].T
            proj, dt_raw = _in_proj(x, norm_mix_g[layer], sc_m, sh_m, w_in_t.astype(BF16), n_cols=ab_main,
                                    n_act_cols=2 * a_width, w_extra=w_in_t[ab_main:], w_transposed=True)
            y_a = _gmlp(proj, gm_ln_g[e], gm_ln_b[e], gm_w_s[e], gm_b_s[e], width=a_width)
            y_b, *ffn16 = _ssd(proj, dt_raw, ssd_conv_w[e], ssd_conv_b[e], ssd_dt_bias[e], ssd_a_log[e], ssd_d[e],
                               ssd_norm_g[e], width=b_width, z_col=2 * a_width, xbc_col=2 * a_width + b_width,
                               cast=ffn_w, cast_layer=layer)
            x = _out_proj(y_a, y_b, ab_w_out[e].astype(BF16), x, g_m)
        else:
            o = layer // 2
            col_scale = jnp.where(jnp.arange(cd_w_in.shape[2]) < c_width, ATT_HEAD_DIM ** -0.5 * LOG2E, 1.0)
            proj = _in_proj(x, norm_mix_g[layer], sc_m, sh_m, (cd_w_in[o] * col_scale).astype(BF16))
            y_c, *ffn16 = _stick_breaking(proj, heads=c_heads, q_col=0, k_col=c_width, v_col=2 * c_width,
                                          cast=ffn_w, cast_layer=layer)
            y_d = _moba(proj, moba_q_norm_g[o], moba_k_norm_g[o], heads=d_heads,
                        q_col=3 * c_width, k_col=3 * c_width + d_width, v_col=3 * c_width + 2 * d_width)
            x = _out_proj(y_c, y_d, cd_w_out[o].astype(BF16), x, g_m, tn=1024)
        x = _ffn(x, norm_ffn_g[layer], sc_f, sh_f, g_f, *ffn16, 0)
    return x
```

```python
import functools

import jax
import jax.numpy as jnp
from jax import lax
from jax.experimental import pallas as pl
from jax.experimental.pallas import tpu as pltpu

F32 = jnp.float32
BF16 = jnp.bfloat16

LANES = 128
A_CHUNK = 128
A_GROUP = 128
SSD_HEAD_DIM = 64
SSD_GROUPS = 4
SSD_STATE = 128
SSD_CONV = 4
SSD_CHUNK = 128
ATT_HEAD_DIM = 128
MOBA_BLOCK = 256
MOBA_TOPK = 3
SB_TILE = 256
SB_DEAD = 160.0
NORM_ROWS = 64
LOG2E = 1.4426950408889634
MIB = 1 << 20


def _params(semantics, vmem_mib):
    return pltpu.CompilerParams(dimension_semantics=semantics, vmem_limit_bytes=vmem_mib * MIB)


def _split_bf16(v):
    hi = v.astype(BF16)
    lo = (v - hi.astype(F32)).astype(BF16)
    return hi, lo


def _split3_bf16(v):
    hi = v.astype(BF16)
    r = v - hi.astype(F32)
    mid = r.astype(BF16)
    lo = (r - mid.astype(F32)).astype(BF16)
    return hi, mid, lo


def _dot(a, b):
    return jnp.dot(a, b, preferred_element_type=F32)


def _dot_nt(a, b):
    return lax.dot_general(a, b, (((1,), (1,)), ((), ())), preferred_element_type=F32)


def _silu(v):
    return v * (0.5 * jnp.tanh(0.5 * v) + 0.5)


def _side_cast_plan(items, grid):
    n_steps = 1
    for g in grid:
        n_steps *= g

    def linear_step(*ids):
        t = ids[0]
        for g, i in zip(grid[1:], ids[1:]):
            t = t * g + i
        return t

    in_specs, out_specs, out_shapes = [], [], []
    for a, layer in items:
        _, rows, cols = a.shape
        rb = next(r for r in range(16, rows + 1, 16) if rows % r == 0 and rows // r <= n_steps)
        last = rows // rb - 1
        in_specs.append(pl.BlockSpec(
            (1, rb, cols), lambda *ids, last=last, layer=layer: (layer, jnp.minimum(linear_step(*ids), last), 0)))
        out_specs.append(pl.BlockSpec(
            (1, rb, cols), lambda *ids, last=last: (0, jnp.minimum(linear_step(*ids), last), 0)))
        out_shapes.append(jax.ShapeDtypeStruct((1, rows, cols), BF16))
    return [a for a, _ in items], in_specs, out_specs, out_shapes


def _side_cast(in_refs, out_refs):
    for i_ref, o_ref in zip(in_refs, out_refs):
        o_ref[...] = i_ref[...].astype(BF16)


def _ada_kernel(c_ref, w_ref, b_ref, o_ref):
    ca = _silu(c_ref[...])
    o_ref[0] = _dot(ca.astype(BF16), w_ref[0].astype(BF16)) + b_ref[0]


def _ada_mod(c, ada_w, ada_b):
    depth, d, n = ada_w.shape
    bsz = c.shape[0]
    tn = 1024
    return pl.pallas_call(
        _ada_kernel,
        out_shape=jax.ShapeDtypeStruct((depth, bsz, n), F32),
        grid=(depth, n // tn),
        in_specs=[
            pl.BlockSpec((bsz, d), lambda l, j: (0, 0)),
            pl.BlockSpec((1, d, tn), lambda l, j: (l, 0, j)),
            pl.BlockSpec((1, 1, tn), lambda l, j: (l, 0, j)),
        ],
        out_specs=pl.BlockSpec((1, bsz, tn), lambda l, j: (l, 0, j)),
        compiler_params=_params(("arbitrary", "arbitrary"), 40),
        name="ada_mod",
    )(c, ada_w, ada_b.reshape(depth, 1, n))


def _norm_chunk(x_ref, scale, shift, h_ref, chunk):
    rows = pl.ds(pl.multiple_of(chunk * NORM_ROWS, NORM_ROWS), NORM_ROWS)
    x = x_ref[0, rows, :]
    rs = lax.rsqrt(jnp.mean(x * x, axis=-1, keepdims=True) + 1e-6)
    h_ref[rows, :] = (x * rs * scale + shift).astype(BF16)


def _norm_mod_store(x_ref, g_ref, sc_ref, sh_ref, h_ref):
    scale = g_ref[...] * (1.0 + sc_ref[0])
    shift = sh_ref[0]

    def body(r, carry):
        _norm_chunk(x_ref, scale, shift, h_ref, r)
        return carry

    lax.fori_loop(0, h_ref.shape[0] // NORM_ROWS, body, 0)


def _norm_ahead(x_ref, g_ref, sc_ref, sh_ref, h_ref, j, n_j):
    n_chunks = h_ref.shape[0] // NORM_ROWS
    per_step = -(-n_chunks // (n_j - 1))
    scale = g_ref[...] * (1.0 + sc_ref[0])
    shift = sh_ref[0]
    for k in range(per_step):
        _norm_chunk(x_ref, scale, shift, h_ref, jnp.clip((j - 1) * per_step + k, 0, n_chunks - 1))


def _ahead_maps(n_b, n_i):
    def tile(b, i, j):
        r = b * n_i + i
        t = jnp.where(jnp.logical_and(r == 0, j == 0), 0, jnp.minimum(r + 1, n_b * n_i - 1))
        return t // n_i, t % n_i

    def x_map(b, i, j):
        tb, ti = tile(b, i, j)
        return tb, ti, 0

    def vec_map(b, i, j):
        return tile(b, i, j)[0], 0, 0

    return x_map, vec_map


def _inproj_kernel(*refs, n_j, n_act_tiles, has_extra, w_transposed):
    if has_extra:
        x_ref, g_ref, sc_ref, sh_ref, w_ref, wx_ref, o_ref, ox_ref, hm_a, hm_b = refs
    else:
        x_ref, g_ref, sc_ref, sh_ref, w_ref, o_ref, hm_a, hm_b = refs
    j = pl.program_id(2)
    r = pl.program_id(0) * pl.num_programs(1) + pl.program_id(1)
    mm = _dot_nt if w_transposed else _dot

    @pl.when(jnp.logical_and(r == 0, j == 0))
    def _():
        _norm_mod_store(x_ref, g_ref, sc_ref, sh_ref, hm_a)

    def tile_step(cur, nxt, act):
        def run():
            if has_extra:
                @pl.when(j == 0)
                def _():
                    ox_ref[0] = mm(cur[...], wx_ref[...])
            _norm_ahead(x_ref, g_ref, sc_ref, sh_ref, nxt, j, n_j)
            acc = mm(cur[...], w_ref[...])
            o_ref[0] = (jax.nn.gelu(acc) if act else acc).astype(o_ref.dtype)
        return run

    for parity, (cur, nxt) in enumerate(((hm_a, hm_b), (hm_b, hm_a))):
        mine = r % 2 == parity
        if n_act_tiles:
            pl.when(jnp.logical_and(mine, j < n_act_tiles))(tile_step(cur, nxt, True))
            pl.when(jnp.logical_and(mine, j >= n_act_tiles))(tile_step(cur, nxt, False))
        else:
            pl.when(mine)(tile_step(cur, nxt, False))


def _in_proj(x, g, sc, sh, w, *, n_cols=None, n_act_cols=0, w_extra=None, w_transposed=False, tm=1024, tn=1024):
    bsz, s, d = x.shape
    n = (w.shape[0] if w_transposed else w.shape[1]) if n_cols is None else n_cols
    tm = min(tm, s)
    assert n % tn == 0 and n_act_cols % tn == 0 and tm % NORM_ROWS == 0 and n // tn >= 2
    has_extra = w_extra is not None
    x_map, vec_map = _ahead_maps(bsz, s // tm)
    in_specs = [
        pl.BlockSpec((1, tm, d), x_map),
        pl.BlockSpec((1, d), lambda b, i, j: (0, 0)),
        pl.BlockSpec((1, 1, d), vec_map),
        pl.BlockSpec((1, 1, d), vec_map),
        pl.BlockSpec((tn, d), lambda b, i, j: (j, 0)) if w_transposed else pl.BlockSpec((d, tn), lambda b, i, j: (0, j)),
    ]
    args = [x, g.reshape(1, d), sc, sh, w]
    out_shape = [jax.ShapeDtypeStruct((bsz, s, n), BF16)]
    out_specs = [pl.BlockSpec((1, tm, tn), lambda b, i, j: (b, i, j))]
    if has_extra:
        axis = 0 if w_transposed else 1
        ne = w_extra.shape[axis]
        assert 2 * ne <= LANES
        pad = [(0, 0), (0, 0)]
        pad[axis] = (0, LANES - 2 * ne)
        wx = jnp.pad(jnp.concatenate(_split_bf16(w_extra), axis=axis), pad)
        in_specs.append(pl.BlockSpec(wx.shape, lambda b, i, j: (0, 0)))
        args.append(wx)
        out_shape.append(jax.ShapeDtypeStruct((bsz, s, LANES), F32))
        out_specs.append(pl.BlockSpec((1, tm, LANES), lambda b, i, j: (b, i, 0)))
    res = pl.pallas_call(
        functools.partial(_inproj_kernel, n_j=n // tn, n_act_tiles=n_act_cols // tn, has_extra=has_extra,
                          w_transposed=w_transposed),
        out_shape=out_shape,
        grid=(bsz, s // tm, n // tn),
        in_specs=in_specs,
        out_specs=out_specs,
        scratch_shapes=[pltpu.VMEM((tm, d), BF16)] * 2,
        compiler_params=_params(("arbitrary", "arbitrary", "arbitrary"), 52),
        name="in_proj",
    )(*args)
    return res if has_extra else res[0]


def _outproj_kernel(ya_ref, yb_ref, wa_ref, wb_ref, x_ref, gate_ref, o_ref):
    acc = _dot(ya_ref[0], wa_ref[...]) + _dot(yb_ref[0], wb_ref[...])
    o_ref[0] = x_ref[0] + (1.0 + gate_ref[0]) * acc


def _out_proj(ya, yb, w, x, gate, *, tm=1024, tn=512):
    bsz, s, d = x.shape
    ka, kb = ya.shape[2], yb.shape[2]
    tm = min(tm, s)
    assert w.shape[0] == ka + kb and ka % kb == 0
    return pl.pallas_call(
        _outproj_kernel,
        out_shape=jax.ShapeDtypeStruct((bsz, s, d), F32),
        grid=(bsz, s // tm, d // tn),
        in_specs=[
            pl.BlockSpec((1, tm, ka), lambda b, i, j: (b, i, 0)),
            pl.BlockSpec((1, tm, kb), lambda b, i, j: (b, i, 0)),
            pl.BlockSpec((ka, tn), lambda b, i, j: (0, j)),
            pl.BlockSpec((kb, tn), lambda b, i, j: (ka // kb, j)),
            pl.BlockSpec((1, tm, tn), lambda b, i, j: (b, i, j)),
            pl.BlockSpec((1, 1, tn), lambda b, i, j: (b, 0, j)),
        ],
        out_specs=pl.BlockSpec((1, tm, tn), lambda b, i, j: (b, i, j)),
        compiler_params=_params(("parallel", "parallel", "arbitrary"), 48),
        name="out_proj",
    )(ya, yb, w, w, x, gate)


def _ffn_kernel(x_ref, g_ref, sc_ref, sh_ref, gate_ref, wg_ref, wu_ref, wd_ref, o_ref, hf_ref):
    f = pl.program_id(2)

    @pl.when(f == 0)
    def _():
        _norm_mod_store(x_ref, g_ref, sc_ref, sh_ref, hf_ref)
        o_ref[0] = jnp.zeros(o_ref.shape[1:], F32)

    hf = hf_ref[...]
    a = _silu(_dot(hf, wg_ref[0])) * _dot(hf, wu_ref[0])
    o_ref[0] += _dot(a.astype(BF16), wd_ref[0])

    @pl.when(f == pl.num_programs(2) - 1)
    def _():
        o_ref[0] = x_ref[0] + (1.0 + gate_ref[0]) * o_ref[0]


def _ffn(x, g, sc, sh, gate, wg, wu, wd, layer, *, tm=512, tf=512):
    bsz, s, d = x.shape
    dff = wg.shape[2]
    tm = min(tm, s)
    assert dff % tf == 0 and tm % NORM_ROWS == 0
    vec = pl.BlockSpec((1, 1, d), lambda b, i, f: (b, 0, 0))
    return pl.pallas_call(
        _ffn_kernel,
        out_shape=jax.ShapeDtypeStruct((bsz, s, d), F32),
        grid=(bsz, s // tm, dff // tf),
        in_specs=[
            pl.BlockSpec((1, tm, d), lambda b, i, f: (b, i, 0)),
            pl.BlockSpec((1, d), lambda b, i, f: (0, 0)),
            vec, vec, vec,
            pl.BlockSpec((1, d, tf), lambda b, i, f: (layer, 0, f)),
            pl.BlockSpec((1, d, tf), lambda b, i, f: (layer, 0, f)),
            pl.BlockSpec((1, tf, d), lambda b, i, f: (layer, f, 0)),
        ],
        out_specs=pl.BlockSpec((1, tm, d), lambda b, i, f: (b, i, 0)),
        scratch_shapes=[pltpu.VMEM((tm, d), BF16)],
        compiler_params=_params(("parallel", "parallel", "arbitrary"), 48),
        name="ffn",
    )(x, g.reshape(1, d), sc, sh, gate, wg, wu, wd)


def _gmlp_kernel(u_ref, v_ref, lng_ref, lnb_ref, ws_ref, bst_ref, o_ref, vln_ref):
    ts, width = v_ref.shape[1], v_ref.shape[2]
    v = v_ref[0].astype(F32)
    mu = jnp.mean(v, axis=-1, keepdims=True)
    vc = v - mu
    var = jnp.mean(vc * vc, axis=-1, keepdims=True)
    vln_ref[...] = (vc * lax.rsqrt(var + 1e-5) * lng_ref[...] + lnb_ref[...]).astype(BF16)
    row = lax.broadcasted_iota(jnp.int32, (A_CHUNK, A_CHUNK), 0)
    col = lax.broadcasted_iota(jnp.int32, (A_CHUNK, A_CHUNK), 1)
    causal = col <= row
    for h in range(width // A_GROUP):
        w = jnp.where(causal, ws_ref[h], 0.0).astype(BF16)
        bias = bst_ref[:, h:h + 1]
        cols = slice(h * A_GROUP, (h + 1) * A_GROUP)
        for c in range(ts // A_CHUNK):
            rows = slice(c * A_CHUNK, (c + 1) * A_CHUNK)
            mixed = _dot(w, vln_ref[rows, cols]) + bias
            o_ref[0, rows, cols] = (u_ref[0, rows, cols].astype(F32) * mixed).astype(o_ref.dtype)


def _gmlp(proj, ln_g, ln_b, w_s, b_s, *, width, ts=512):
    bsz, s, _ = proj.shape
    heads = w_s.shape[0]
    ts = min(ts, s)
    return pl.pallas_call(
        _gmlp_kernel,
        out_shape=jax.ShapeDtypeStruct((bsz, s, width), BF16),
        grid=(bsz, s // ts),
        in_specs=[
            pl.BlockSpec((1, ts, width), lambda b, i: (b, i, 0)),
            pl.BlockSpec((1, ts, width), lambda b, i: (b, i, 1)),
            pl.BlockSpec((1, width), lambda b, i: (0, 0)),
            pl.BlockSpec((1, width), lambda b, i: (0, 0)),
            pl.BlockSpec((heads, A_CHUNK, A_CHUNK), lambda b, i: (0, 0, 0)),
            pl.BlockSpec((A_CHUNK, heads), lambda b, i: (0, 0)),
        ],
        out_specs=pl.BlockSpec((1, ts, width), lambda b, i: (b, i, 0)),
        scratch_shapes=[pltpu.VMEM((ts, width), BF16)],
        compiler_params=_params(("parallel", "parallel"), 40),
        name="gmlp",
    )(proj, proj, ln_g.reshape(1, width), ln_b.reshape(1, width), w_s, b_s.T)


def _pair_bcast(mat, q, low_half):
    rows = mat.shape[0]
    a = jnp.broadcast_to(mat[:, 2 * q:2 * q + 1], (rows, LANES))
    b = jnp.broadcast_to(mat[:, 2 * q + 1:2 * q + 2], (rows, LANES))
    return jnp.where(low_half, a, b)


def _ssd_kernel(*refs, width, n_bc, n_cast):
    z_ref, xbc_ref, dt_ref, cw_ref, cb_ref, dtb_ref, alog_ref, dskip_ref, ng_ref = refs[:9]
    o_ref = refs[9 + n_cast]
    xpad_ref, xs_ref, state_ref, y_ref = refs[10 + 2 * n_cast:]
    _side_cast(refs[9:9 + n_cast], refs[10 + n_cast:10 + 2 * n_cast])
    L = SSD_CHUNK
    c = pl.program_id(1)
    pad = 8

    @pl.when(c == 0)
    def _():
        xpad_ref[0:pad, :] = jnp.zeros((pad, xpad_ref.shape[1]), F32)
        state_ref[...] = jnp.zeros_like(state_ref)

    @pl.when(c > 0)
    def _():
        xpad_ref[0:pad, :] = xpad_ref[L:L + pad, :]

    xpad_ref[pad:pad + L, :] = xbc_ref[0].astype(F32)

    conv = cb_ref[...] + cw_ref[0:1, :] * xpad_ref[pad - 3:pad - 3 + L, :]
    for k in range(1, SSD_CONV):
        conv = conv + cw_ref[k:k + 1, :] * xpad_ref[pad - 3 + k:pad - 3 + k + L, :]
    xs_ref[...] = _silu(conv)

    heads = width // SSD_HEAD_DIM
    dt_raw = dt_ref[0]
    dt_raw = dt_raw + pltpu.roll(dt_raw, LANES - heads, axis=1)
    dt = jax.nn.softplus(dt_raw + dtb_ref[...])
    a_dt = dt * (-jnp.exp(alog_ref[...]))
    row = lax.broadcasted_iota(jnp.int32, (L, L), 0)
    col = lax.broadcasted_iota(jnp.int32, (L, L), 1)
    causal = col <= row
    tri = jnp.where(causal, 1.0, 0.0).astype(BF16)
    p0, p1, p2 = _split3_bf16(a_dt)
    a_cum = _dot(tri, p0) + _dot(tri, p1) + _dot(tri, p2)
    a_cum_t = a_cum.T
    a_last = a_cum[L - 1:L, :]
    dt_t = dt.T
    e_cum = jnp.exp(a_cum)
    dt_end = dt * jnp.exp(a_last - a_cum)

    low_half = lax.broadcasted_iota(jnp.int32, (L, LANES), 1) < SSD_HEAD_DIM
    heads_per_group = width // SSD_HEAD_DIM // SSD_GROUPS
    pairs_per_group = heads_per_group // 2
    gw = width // SSD_GROUPS
    for g in range(SSD_GROUPS):
        bm = xs_ref[:, width + g * SSD_STATE:width + (g + 1) * SSD_STATE]
        cm = xs_ref[:, width + n_bc + g * SSD_STATE:width + n_bc + (g + 1) * SSD_STATE]
        bm16 = bm.astype(BF16)
        cm16 = cm.astype(BF16)
        cb = _dot_nt(cm16, bm16)
        y_off_g = _dot(cm16, state_ref[:, g * gw:(g + 1) * gw].astype(BF16))
        bm_t16 = bm.T.astype(BF16)
        for pq in range(pairs_per_group):
            q = g * pairs_per_group + pq
            lanes = slice(q * LANES, (q + 1) * LANES)
            xs_p = xs_ref[:, lanes]
            ms = []
            for r in (2 * q, 2 * q + 1):
                seg = a_cum[:, r:r + 1] - a_cum_t[r:r + 1, :]
                decay = jnp.exp(jnp.where(causal, seg, -jnp.inf))
                ms.append((cb * decay * dt_t[r:r + 1, :]).astype(BF16))
            xs16 = xs_p.astype(BF16)
            zero = jnp.zeros_like(xs16)
            rhs = jnp.concatenate([jnp.where(low_half, xs16, zero), jnp.where(low_half, zero, xs16)], axis=0)
            y_diag = _dot(jnp.concatenate(ms, axis=1), rhs)
            e_p = _pair_bcast(e_cum, q, low_half)
            y_p = y_diag + y_off_g[:, pq * LANES:(pq + 1) * LANES] * e_p + dskip_ref[:, lanes] * xs_p
            y_ref[:, lanes] = y_p
            xdec16 = (xs_p * _pair_bcast(dt_end, q, low_half)).astype(BF16)
            s_new = _dot(bm_t16, xdec16)
            state_ref[:, lanes] = state_ref[:, lanes] * e_p[L - 1:L, :] + s_new

    z = z_ref[0].astype(F32)
    yz = y_ref[...] * _silu(z)
    ms = jnp.mean(yz * yz, axis=-1, keepdims=True)
    o_ref[0] = (yz * lax.rsqrt(ms + 1e-6) * ng_ref[...]).astype(o_ref.dtype)


def _ssd(proj, dt_raw, conv_w, conv_b, dt_bias, a_log, d_skip, norm_g, *, width, z_col, xbc_col, cast=()):
    bsz, s, _ = proj.shape
    L = SSD_CHUNK
    conv_ch = conv_w.shape[1]
    n_bc = SSD_GROUPS * SSD_STATE
    heads = width // SSD_HEAD_DIM
    assert conv_ch == width + 2 * n_bc and z_col % width == 0 and xbc_col % conv_ch == 0

    def padl(v):
        return jnp.pad(v.astype(F32), (0, LANES - heads)).reshape(1, LANES)

    grid = (bsz, s // L)
    cast_arrays, cast_in, cast_out, cast_shapes = _side_cast_plan(cast, grid)
    const = lambda shape: pl.BlockSpec(shape, lambda b, c: (0,) * len(shape))
    return pl.pallas_call(
        functools.partial(_ssd_kernel, width=width, n_bc=n_bc, n_cast=len(cast)),
        out_shape=[jax.ShapeDtypeStruct((bsz, s, width), BF16)] + cast_shapes,
        grid=grid,
        in_specs=[
            pl.BlockSpec((1, L, width), lambda b, c: (b, c, z_col // width)),
            pl.BlockSpec((1, L, conv_ch), lambda b, c: (b, c, xbc_col // conv_ch)),
            pl.BlockSpec((1, L, LANES), lambda b, c: (b, c, 0)),
            const((SSD_CONV, conv_ch)),
            const((1, conv_ch)),
            const((1, LANES)),
            const((1, LANES)),
            const((1, width)),
            const((1, width)),
        ] + cast_in,
        out_specs=[pl.BlockSpec((1, L, width), lambda b, c: (b, c, 0))] + cast_out,
        scratch_shapes=[
            pltpu.VMEM((L + 8, conv_ch), F32),
            pltpu.VMEM((L, conv_ch), F32),
            pltpu.VMEM((SSD_STATE, width), F32),
            pltpu.VMEM((L, width), F32),
        ],
        compiler_params=_params(("parallel", "arbitrary"), 40),
        name="ssd",
    )(proj, proj, dt_raw, conv_w, conv_b.reshape(1, conv_ch), padl(dt_bias), padl(a_log),
      jnp.repeat(d_skip.astype(F32), SSD_HEAD_DIM).reshape(1, width), norm_g.reshape(1, width), *cast_arrays)


def _sb_kernel(*refs, heads, n_cast):
    q_ref, k_ref, v_ref = refs[:3]
    o_ref = refs[3 + n_cast]
    _side_cast(refs[3:3 + n_cast], refs[4 + n_cast:])
    _sb_tile(q_ref, k_ref, v_ref, o_ref, heads=heads)


def _sb_tile(q_ref, k_ref, v_ref, o_ref, *, heads):
    T = SB_TILE
    dh = ATT_HEAD_DIM
    i = pl.program_id(2)
    row = lax.broadcasted_iota(jnp.int32, (T, T), 0)
    col = lax.broadcasted_iota(jnp.int32, (T, T), 1)
    past = col < row
    suffix = jnp.where(row >= col, 1.0, 0.0).astype(BF16)

    def step(j, state, diagonal):
        start = pl.multiple_of(j * T, T)
        cols = [slice(h * dh, (h + 1) * dh) for h in range(heads)]
        logits = [_dot_nt(q_ref[0, :, c], k_ref[0, pl.ds(start, T), c]) for c in cols]
        cost = [jnp.maximum(x, 0.0) + jnp.log2(1.0 + jnp.exp2(-jnp.abs(x))) for x in logits]
        if diagonal:
            cost = [jnp.where(past, c, 0.0) for c in cost]
        rinc = [_dot(c.astype(BF16), suffix) for c in cost]
        w = [jnp.exp2(jnp.minimum(x - r, 0.0) - st[0]) for x, r, st in zip(logits, rinc, state)]
        if diagonal:
            w = [jnp.where(past, wh, 0.0) for wh in w]
        acc = [st[1] + _dot(wh.astype(BF16), v_ref[0, pl.ds(start, T), c]) for wh, st, c in zip(w, state, cols)]
        return tuple((st[0] + r[:, 0:1], a) for st, r, a in zip(state, rinc, acc))

    zero = (jnp.zeros((T, 1), F32), jnp.zeros((T, dh), F32))
    state = step(i, (zero,) * heads, True)

    def body(loop):
        n, _, st = loop
        st = step(i - 1 - n, st, False)
        lowest = functools.reduce(jnp.minimum, [s[0] for s in st])
        return n + 1, jnp.min(lowest), st

    def live(loop):
        n, lowest, _ = loop
        return jnp.logical_and(n < i, lowest < SB_DEAD)

    _, _, state = lax.while_loop(live, body, (jnp.int32(0), jnp.float32(0.0), state))
    for h in range(heads):
        o_ref[0, :, h * dh:(h + 1) * dh] = state[h][1].astype(o_ref.dtype)


def _stick_breaking(proj, *, heads, q_col, k_col, v_col, heads_per_step=6, cast=()):
    bsz, s, _ = proj.shape
    T = SB_TILE
    hb = heads_per_step
    wb = hb * ATT_HEAD_DIM
    assert heads % hb == 0 and q_col % wb == 0 and k_col % wb == 0 and v_col % wb == 0 and s % T == 0
    grid = (bsz, heads // hb, s // T)
    cast_arrays, cast_in, cast_out, cast_shapes = _side_cast_plan(cast, grid)
    return pl.pallas_call(
        functools.partial(_sb_kernel, heads=hb, n_cast=len(cast)),
        out_shape=[jax.ShapeDtypeStruct((bsz, s, heads * ATT_HEAD_DIM), BF16)] + cast_shapes,
        grid=grid,
        in_specs=[
            pl.BlockSpec((1, T, wb), lambda b, h, i: (b, i, q_col // wb + h)),
            pl.BlockSpec((1, s, wb), lambda b, h, i: (b, 0, k_col // wb + h)),
            pl.BlockSpec((1, s, wb), lambda b, h, i: (b, 0, v_col // wb + h)),
        ] + cast_in,
        out_specs=[pl.BlockSpec((1, T, wb), lambda b, h, i: (b, i, h))] + cast_out,
        compiler_params=_params(("arbitrary", "arbitrary", "arbitrary"), 40),
        name="stick_breaking",
    )(proj, proj, proj, *cast_arrays)


def _head_rms(v, g):
    ms = jnp.mean(v * v, axis=-1, keepdims=True)
    return v * lax.rsqrt(ms + 1e-6) * g


def _moba_kernel(q_ref, k_ref, v_ref, qg_ref, kg_ref, o_ref, kn_ref, kmh_ref, kml_ref, *, scale, n_blk):
    BLK = MOBA_BLOCK
    i = pl.program_id(2)

    @pl.when(i == 0)
    def _():
        kmh_ref[...] = jnp.zeros_like(kmh_ref)
        kml_ref[...] = jnp.zeros_like(kml_ref)
        for n in range(n_blk):
            rows = slice(n * BLK, (n + 1) * BLK)
            kn = _head_rms(k_ref[0, rows, :].astype(F32), kg_ref[...])
            kn_ref[rows, :] = kn.astype(BF16)
            hi, lo = _split_bf16(jnp.mean(kn, axis=0, keepdims=True))
            kmh_ref[n:n + 1, :] = hi
            kml_ref[n:n + 1, :] = lo

    qn = _head_rms(q_ref[0].astype(F32), qg_ref[...])
    q16 = (qn * scale).astype(BF16)
    nb_pad = -(-n_blk // 8) * 8
    blk = lax.broadcasted_iota(jnp.int32, (nb_pad, BLK), 0)
    row = lax.broadcasted_iota(jnp.int32, (BLK, BLK), 0)
    col = lax.broadcasted_iota(jnp.int32, (BLK, BLK), 1)

    def q_block(iv):
        rows = lambda n: slice(n * BLK, (n + 1) * BLK)
        scores = [jnp.where(col <= row, _dot_nt(q16, kn_ref[rows(iv), :]), -jnp.inf)]
        if iv > MOBA_TOPK:
            qh, ql = _split_bf16(qn)
            kmh = kmh_ref[...]
            gate = (_dot_nt(kmh, qh) + _dot_nt(kmh, ql) + _dot_nt(kml_ref[...], qh))[0:nb_pad]
            gate = jnp.where(blk < iv, gate, -jnp.inf)
            bias = jnp.zeros((nb_pad, BLK), F32)
            for n in range(iv):
                gn = gate[n:n + 1, :]
                ahead = (gate > gn) | ((gate == gn) & (blk < n))
                rank = jnp.sum(jnp.where(ahead, 1.0, 0.0), axis=0, keepdims=True)
                bias = jnp.where((blk == n) & (rank >= MOBA_TOPK), -jnp.inf, bias)
            bias = jnp.concatenate([bias, jnp.zeros((LANES - nb_pad, BLK), F32)], axis=0).T
        for n in range(iv):
            s = _dot_nt(q16, kn_ref[rows(n), :])
            if iv > MOBA_TOPK:
                s = s + bias[:, n:n + 1]
            scores.append(s)
        m = functools.reduce(jnp.maximum, [jnp.max(s, axis=-1, keepdims=True) for s in scores])
        l = 0.0
        acc = 0.0
        for s, n in zip(scores, [iv] + list(range(iv))):
            p = jnp.exp(s - m)
            l = l + jnp.sum(p, axis=-1, keepdims=True)
            acc = acc + _dot(p.astype(BF16), v_ref[0, rows(n), :])
        o_ref[0] = (acc / l).astype(o_ref.dtype)

    for iv in range(n_blk):
        pl.when(i == iv)(functools.partial(q_block, iv))


def _moba(proj, q_norm_g, k_norm_g, *, heads, q_col, k_col, v_col):
    bsz, s, _ = proj.shape
    dh = ATT_HEAD_DIM
    BLK = MOBA_BLOCK
    n_blk = s // BLK
    assert s % BLK == 0 and n_blk <= LANES
    return pl.pallas_call(
        functools.partial(_moba_kernel, scale=dh ** -0.5, n_blk=n_blk),
        out_shape=jax.ShapeDtypeStruct((bsz, s, heads * dh), BF16),
        grid=(bsz, heads, n_blk),
        in_specs=[
            pl.BlockSpec((1, BLK, dh), lambda b, h, i: (b, i, q_col // dh + h)),
            pl.BlockSpec((1, s, dh), lambda b, h, i: (b, 0, k_col // dh + h)),
            pl.BlockSpec((1, s, dh), lambda b, h, i: (b, 0, v_col // dh + h)),
            pl.BlockSpec((1, dh), lambda b, h, i: (0, 0)),
            pl.BlockSpec((1, dh), lambda b, h, i: (0, 0)),
        ],
        out_specs=pl.BlockSpec((1, BLK, dh), lambda b, h, i: (b, i, h)),
        scratch_shapes=[
            pltpu.VMEM((s, dh), BF16),
            pltpu.VMEM((LANES, dh), BF16),
            pltpu.VMEM((LANES, dh), BF16),
        ],
        compiler_params=_params(("parallel", "parallel", "arbitrary"), 32),
        name="moba",
    )(proj, proj, proj, q_norm_g.reshape(1, dh), k_norm_g.reshape(1, dh))


def kernel(x, c, norm_mix_g, norm_ffn_g, ada_w, ada_b, ffn_w_gate, ffn_w_up, ffn_w_down, ab_w_in, ab_w_out, gm_ln_g, gm_ln_b, gm_w_s, gm_b_s, ssd_conv_w, ssd_conv_b, ssd_dt_bias, ssd_a_log, ssd_d, ssd_norm_g, cd_w_in, cd_w_out, moba_q_norm_g, moba_k_norm_g):
    bsz, s, d = x.shape
    depth = ada_w.shape[0]
    mod = _ada_mod(c, ada_w, ada_b)

    a_width = gm_ln_g.shape[1]
    b_width = ssd_norm_g.shape[1]
    conv_ch = ssd_conv_w.shape[2]
    ssd_heads = ssd_dt_bias.shape[1]
    ab_main = 2 * a_width + b_width + conv_ch
    n_att = cd_w_out.shape[1] // ATT_HEAD_DIM
    d_heads = n_att // 4
    c_heads = n_att - d_heads
    c_width = c_heads * ATT_HEAD_DIM
    d_width = d_heads * ATT_HEAD_DIM

    ffn_w = (ffn_w_gate, ffn_w_up, ffn_w_down)
    for layer in range(depth):
        sh_m, sc_m, g_m, sh_f, sc_f, g_f = [mod[layer, :, None, k * d:(k + 1) * d] for k in range(6)]
        if layer % 2 == 0:
            e = layer // 2
            w_in_t = ab_w_in[e].T
            proj, dt_raw = _in_proj(x, norm_mix_g[layer], sc_m, sh_m, w_in_t.astype(BF16), n_cols=ab_main,
                                    n_act_cols=2 * a_width, w_extra=w_in_t[ab_main:], w_transposed=True)
            y_a = _gmlp(proj, gm_ln_g[e], gm_ln_b[e], gm_w_s[e], gm_b_s[e], width=a_width)
            y_b, w_out16, *ffn16 = _ssd(
                proj, dt_raw, ssd_conv_w[e], ssd_conv_b[e], ssd_dt_bias[e], ssd_a_log[e], ssd_d[e], ssd_norm_g[e],
                width=b_width, z_col=2 * a_width, xbc_col=2 * a_width + b_width,
                cast=[(ab_w_out, e)] + [(w, layer) for w in ffn_w])
            x = _out_proj(y_a, y_b, w_out16[0], x, g_m)
        else:
            o = layer // 2
            col_scale = jnp.where(jnp.arange(cd_w_in.shape[2]) < c_width, ATT_HEAD_DIM ** -0.5 * LOG2E, 1.0)
            proj = _in_proj(x, norm_mix_g[layer], sc_m, sh_m, (cd_w_in[o] * col_scale).astype(BF16))
            y_c, w_out16, *ffn16 = _stick_breaking(
                proj, heads=c_heads, q_col=0, k_col=c_width, v_col=2 * c_width,
                cast=[(cd_w_out, o)] + [(w, layer) for w in ffn_w])
            y_d = _moba(proj, moba_q_norm_g[o], moba_k_norm_g[o], heads=d_heads,
                        q_col=3 * c_width, k_col=3 * c_width + d_width, v_col=3 * c_width + 2 * d_width)
            x = _out_proj(y_c, y_d, w_out16[0], x, g_m, tn=1024)
        x = _ffn(x, norm_ffn_g[layer], sc_f, sh_f, g_f, *ffn16, 0)
    return x
```

```python
import functools

import jax
import jax.numpy as jnp
from jax import lax
from jax.experimental import pallas as pl
from jax.experimental.pallas import tpu as pltpu

F32 = jnp.float32
BF16 = jnp.bfloat16

LANES = 128
A_CHUNK = 128
A_GROUP = 128
SSD_HEAD_DIM = 64
SSD_GROUPS = 4
SSD_STATE = 128
SSD_CONV = 4
SSD_CHUNK = 128
ATT_HEAD_DIM = 128
MOBA_BLOCK = 256
MOBA_TOPK = 3
SB_TILE = 256
SB_DEAD = 160.0
NORM_ROWS = 64
LOG2E = 1.4426950408889634
MIB = 1 << 20
V7X_VMEM_MIB = 64
VMEM_LIMIT_MIB = {"ada_mod": 40, "in_proj": 52, "out_proj": 48, "ffn": 48, "gmlp": 40, "ssd": 40,
                  "stick_breaking": 40, "moba": 32}
assert max(VMEM_LIMIT_MIB.values()) < V7X_VMEM_MIB


def _params(semantics, name):
    return pltpu.CompilerParams(dimension_semantics=semantics, vmem_limit_bytes=VMEM_LIMIT_MIB[name] * MIB)


def _split_bf16(v):
    hi = v.astype(BF16)
    lo = (v - hi.astype(F32)).astype(BF16)
    return hi, lo


def _split3_bf16(v):
    hi = v.astype(BF16)
    r = v - hi.astype(F32)
    mid = r.astype(BF16)
    lo = (r - mid.astype(F32)).astype(BF16)
    return hi, mid, lo


def _dot(a, b):
    return jnp.dot(a, b, preferred_element_type=F32)


def _dot_nt(a, b):
    return lax.dot_general(a, b, (((1,), (1,)), ((), ())), preferred_element_type=F32)


def _silu(v):
    return v * (0.5 * jnp.tanh(0.5 * v) + 0.5)


def _side_cast_plan(items, grid):
    n_steps = 1
    for g in grid:
        n_steps *= g

    def linear_step(*ids):
        t = ids[0]
        for g, i in zip(grid[1:], ids[1:]):
            t = t * g + i
        return t

    in_specs, out_specs, out_shapes = [], [], []
    for a, layer in items:
        _, rows, cols = a.shape
        rb = next(r for r in range(16, rows + 1, 16) if rows % r == 0 and rows // r <= n_steps)
        last = rows // rb - 1
        in_specs.append(pl.BlockSpec(
            (1, rb, cols), lambda *ids, last=last, layer=layer: (layer, jnp.minimum(linear_step(*ids), last), 0)))
        out_specs.append(pl.BlockSpec(
            (1, rb, cols), lambda *ids, last=last: (0, jnp.minimum(linear_step(*ids), last), 0)))
        out_shapes.append(jax.ShapeDtypeStruct((1, rows, cols), BF16))
    return [a for a, _ in items], in_specs, out_specs, out_shapes


def _side_cast(in_refs, out_refs):
    for i_ref, o_ref in zip(in_refs, out_refs):
        o_ref[...] = i_ref[...].astype(BF16)


def _ada_kernel(c_ref, w_ref, b_ref, o_ref):
    ca = _silu(c_ref[...])
    o_ref[0] = _dot(ca.astype(BF16), w_ref[0].astype(BF16)) + b_ref[0]


def _ada_mod(c, ada_w, ada_b):
    depth, d, n = ada_w.shape
    bsz = c.shape[0]
    tn = 1024
    return pl.pallas_call(
        _ada_kernel,
        out_shape=jax.ShapeDtypeStruct((depth, bsz, n), F32),
        grid=(depth, n // tn),
        in_specs=[
            pl.BlockSpec((bsz, d), lambda l, j: (0, 0)),
            pl.BlockSpec((1, d, tn), lambda l, j: (l, 0, j)),
            pl.BlockSpec((1, 1, tn), lambda l, j: (l, 0, j)),
        ],
        out_specs=pl.BlockSpec((1, bsz, tn), lambda l, j: (l, 0, j)),
        compiler_params=_params(("arbitrary", "arbitrary"), "ada_mod"),
        name="ada_mod",
    )(c, ada_w, ada_b.reshape(depth, 1, n))


def _norm_chunk(x_ref, scale, shift, h_ref, chunk):
    rows = pl.ds(pl.multiple_of(chunk * NORM_ROWS, NORM_ROWS), NORM_ROWS)
    x = x_ref[0, rows, :]
    rs = lax.rsqrt(jnp.mean(x * x, axis=-1, keepdims=True) + 1e-6)
    h_ref[rows, :] = (x * rs * scale + shift).astype(BF16)


def _norm_mod_store(x_ref, g_ref, sc_ref, sh_ref, h_ref):
    scale = g_ref[...] * (1.0 + sc_ref[0])
    shift = sh_ref[0]

    def body(r, carry):
        _norm_chunk(x_ref, scale, shift, h_ref, r)
        return carry

    lax.fori_loop(0, h_ref.shape[0] // NORM_ROWS, body, 0)


def _norm_ahead(x_ref, g_ref, sc_ref, sh_ref, h_ref, j, n_j):
    n_chunks = h_ref.shape[0] // NORM_ROWS
    per_step = -(-n_chunks // (n_j - 1))
    scale = g_ref[...] * (1.0 + sc_ref[0])
    shift = sh_ref[0]
    for k in range(per_step):
        _norm_chunk(x_ref, scale, shift, h_ref, jnp.clip((j - 1) * per_step + k, 0, n_chunks - 1))


def _ahead_maps(n_b, n_i):
    def tile(b, i, j):
        r = b * n_i + i
        t = jnp.where(jnp.logical_and(r == 0, j == 0), 0, jnp.minimum(r + 1, n_b * n_i - 1))
        return t // n_i, t % n_i

    def x_map(b, i, j):
        tb, ti = tile(b, i, j)
        return tb, ti, 0

    def vec_map(b, i, j):
        return tile(b, i, j)[0], 0, 0

    return x_map, vec_map


def _inproj_kernel(*refs, n_j, n_act_tiles, has_extra, w_transposed):
    if has_extra:
        x_ref, g_ref, sc_ref, sh_ref, w_ref, wx_ref, o_ref, ox_ref, hm_a, hm_b = refs
    else:
        x_ref, g_ref, sc_ref, sh_ref, w_ref, o_ref, hm_a, hm_b = refs
    j = pl.program_id(2)
    r = pl.program_id(0) * pl.num_programs(1) + pl.program_id(1)
    mm = _dot_nt if w_transposed else _dot

    @pl.when(jnp.logical_and(r == 0, j == 0))
    def _():
        _norm_mod_store(x_ref, g_ref, sc_ref, sh_ref, hm_a)

    def tile_step(cur, nxt, act):
        def run():
            if has_extra:
                @pl.when(j == 0)
                def _():
                    ox_ref[0] = mm(cur[...], wx_ref[...])
            _norm_ahead(x_ref, g_ref, sc_ref, sh_ref, nxt, j, n_j)
            acc = mm(cur[...], w_ref[...])
            o_ref[0] = (jax.nn.gelu(acc) if act else acc).astype(o_ref.dtype)
        return run

    for parity, (cur, nxt) in enumerate(((hm_a, hm_b), (hm_b, hm_a))):
        mine = r % 2 == parity
        if n_act_tiles:
            pl.when(jnp.logical_and(mine, j < n_act_tiles))(tile_step(cur, nxt, True))
            pl.when(jnp.logical_and(mine, j >= n_act_tiles))(tile_step(cur, nxt, False))
        else:
            pl.when(mine)(tile_step(cur, nxt, False))


def _in_proj(x, g, sc, sh, w, *, n_cols=None, n_act_cols=0, w_extra=None, w_transposed=False, tm=1024, tn=1024):
    bsz, s, d = x.shape
    n = (w.shape[0] if w_transposed else w.shape[1]) if n_cols is None else n_cols
    tm = min(tm, s)
    assert n % tn == 0 and n_act_cols % tn == 0 and tm % NORM_ROWS == 0 and n // tn >= 2
    has_extra = w_extra is not None
    x_map, vec_map = _ahead_maps(bsz, s // tm)
    in_specs = [
        pl.BlockSpec((1, tm, d), x_map),
        pl.BlockSpec((1, d), lambda b, i, j: (0, 0)),
        pl.BlockSpec((1, 1, d), vec_map),
        pl.BlockSpec((1, 1, d), vec_map),
        pl.BlockSpec((tn, d), lambda b, i, j: (j, 0)) if w_transposed else pl.BlockSpec((d, tn), lambda b, i, j: (0, j)),
    ]
    args = [x, g.reshape(1, d), sc, sh, w]
    out_shape = [jax.ShapeDtypeStruct((bsz, s, n), BF16)]
    out_specs = [pl.BlockSpec((1, tm, tn), lambda b, i, j: (b, i, j))]
    if has_extra:
        axis = 0 if w_transposed else 1
        ne = w_extra.shape[axis]
        assert 2 * ne <= LANES
        pad = [(0, 0), (0, 0)]
        pad[axis] = (0, LANES - 2 * ne)
        wx = jnp.pad(jnp.concatenate(_split_bf16(w_extra), axis=axis), pad)
        in_specs.append(pl.BlockSpec(wx.shape, lambda b, i, j: (0, 0)))
        args.append(wx)
        out_shape.append(jax.ShapeDtypeStruct((bsz, s, LANES), F32))
        out_specs.append(pl.BlockSpec((1, tm, LANES), lambda b, i, j: (b, i, 0)))
    res = pl.pallas_call(
        functools.partial(_inproj_kernel, n_j=n // tn, n_act_tiles=n_act_cols // tn, has_extra=has_extra,
                          w_transposed=w_transposed),
        out_shape=out_shape,
        grid=(bsz, s // tm, n // tn),
        in_specs=in_specs,
        out_specs=out_specs,
        scratch_shapes=[pltpu.VMEM((tm, d), BF16)] * 2,
        compiler_params=_params(("arbitrary", "arbitrary", "arbitrary"), "in_proj"),
        name="in_proj",
    )(*args)
    return res if has_extra else res[0]


def _outproj_kernel(ya_ref, yb_ref, wa_ref, wb_ref, x_ref, gate_ref, o_ref):
    acc = _dot(ya_ref[0], wa_ref[...]) + _dot(yb_ref[0], wb_ref[...])
    o_ref[0] = x_ref[0] + (1.0 + gate_ref[0]) * acc


def _out_proj(ya, yb, w, x, gate, *, tm=1024, tn=512):
    bsz, s, d = x.shape
    ka, kb = ya.shape[2], yb.shape[2]
    tm = min(tm, s)
    assert w.shape[0] == ka + kb and ka % kb == 0
    return pl.pallas_call(
        _outproj_kernel,
        out_shape=jax.ShapeDtypeStruct((bsz, s, d), F32),
        grid=(bsz, s // tm, d // tn),
        in_specs=[
            pl.BlockSpec((1, tm, ka), lambda b, i, j: (b, i, 0)),
            pl.BlockSpec((1, tm, kb), lambda b, i, j: (b, i, 0)),
            pl.BlockSpec((ka, tn), lambda b, i, j: (0, j)),
            pl.BlockSpec((kb, tn), lambda b, i, j: (ka // kb, j)),
            pl.BlockSpec((1, tm, tn), lambda b, i, j: (b, i, j)),
            pl.BlockSpec((1, 1, tn), lambda b, i, j: (b, 0, j)),
        ],
        out_specs=pl.BlockSpec((1, tm, tn), lambda b, i, j: (b, i, j)),
        compiler_params=_params(("parallel", "parallel", "arbitrary"), "out_proj"),
        name="out_proj",
    )(ya, yb, w, w, x, gate)


def _ffn_kernel(x_ref, g_ref, sc_ref, sh_ref, gate_ref, wg_ref, wu_ref, wd_ref, o_ref, hf_ref):
    f = pl.program_id(2)

    @pl.when(f == 0)
    def _():
        _norm_mod_store(x_ref, g_ref, sc_ref, sh_ref, hf_ref)
        o_ref[0] = jnp.zeros(o_ref.shape[1:], F32)

    hf = hf_ref[...]
    a = _silu(_dot(hf, wg_ref[0])) * _dot(hf, wu_ref[0])
    o_ref[0] += _dot(a.astype(BF16), wd_ref[0])

    @pl.when(f == pl.num_programs(2) - 1)
    def _():
        o_ref[0] = x_ref[0] + (1.0 + gate_ref[0]) * o_ref[0]


def _ffn(x, g, sc, sh, gate, wg, wu, wd, layer, *, tm=512, tf=512):
    bsz, s, d = x.shape
    dff = wg.shape[2]
    tm = min(tm, s)
    assert dff % tf == 0 and tm % NORM_ROWS == 0
    vec = pl.BlockSpec((1, 1, d), lambda b, i, f: (b, 0, 0))
    return pl.pallas_call(
        _ffn_kernel,
        out_shape=jax.ShapeDtypeStruct((bsz, s, d), F32),
        grid=(bsz, s // tm, dff // tf),
        in_specs=[
            pl.BlockSpec((1, tm, d), lambda b, i, f: (b, i, 0)),
            pl.BlockSpec((1, d), lambda b, i, f: (0, 0)),
            vec, vec, vec,
            pl.BlockSpec((1, d, tf), lambda b, i, f: (layer, 0, f)),
            pl.BlockSpec((1, d, tf), lambda b, i, f: (layer, 0, f)),
            pl.BlockSpec((1, tf, d), lambda b, i, f: (layer, f, 0)),
        ],
        out_specs=pl.BlockSpec((1, tm, d), lambda b, i, f: (b, i, 0)),
        scratch_shapes=[pltpu.VMEM((tm, d), BF16)],
        compiler_params=_params(("parallel", "parallel", "arbitrary"), "ffn"),
        name="ffn",
    )(x, g.reshape(1, d), sc, sh, gate, wg, wu, wd)


def _gmlp_kernel(u_ref, v_ref, lng_ref, lnb_ref, ws_ref, bst_ref, o_ref, vln_ref):
    ts, width = v_ref.shape[1], v_ref.shape[2]
    v = v_ref[0].astype(F32)
    mu = jnp.mean(v, axis=-1, keepdims=True)
    vc = v - mu
    var = jnp.mean(vc * vc, axis=-1, keepdims=True)
    vln_ref[...] = (vc * lax.rsqrt(var + 1e-5) * lng_ref[...] + lnb_ref[...]).astype(BF16)
    row = lax.broadcasted_iota(jnp.int32, (A_CHUNK, A_CHUNK), 0)
    col = lax.broadcasted_iota(jnp.int32, (A_CHUNK, A_CHUNK), 1)
    causal = col <= row
    for h in range(width // A_GROUP):
        w = jnp.where(causal, ws_ref[h], 0.0).astype(BF16)
        bias = bst_ref[:, h:h + 1]
        cols = slice(h * A_GROUP, (h + 1) * A_GROUP)
        for c in range(ts // A_CHUNK):
            rows = slice(c * A_CHUNK, (c + 1) * A_CHUNK)
            mixed = _dot(w, vln_ref[rows, cols]) + bias
            o_ref[0, rows, cols] = (u_ref[0, rows, cols].astype(F32) * mixed).astype(o_ref.dtype)


def _gmlp(proj, ln_g, ln_b, w_s, b_s, *, width, ts=1024):
    bsz, s, _ = proj.shape
    heads = w_s.shape[0]
    ts = min(ts, s)
    return pl.pallas_call(
        _gmlp_kernel,
        out_shape=jax.ShapeDtypeStruct((bsz, s, width), BF16),
        grid=(bsz, s // ts),
        in_specs=[
            pl.BlockSpec((1, ts, width), lambda b, i: (b, i, 0)),
            pl.BlockSpec((1, ts, width), lambda b, i: (b, i, 1)),
            pl.BlockSpec((1, width), lambda b, i: (0, 0)),
            pl.BlockSpec((1, width), lambda b, i: (0, 0)),
            pl.BlockSpec((heads, A_CHUNK, A_CHUNK), lambda b, i: (0, 0, 0)),
            pl.BlockSpec((A_CHUNK, heads), lambda b, i: (0, 0)),
        ],
        out_specs=pl.BlockSpec((1, ts, width), lambda b, i: (b, i, 0)),
        scratch_shapes=[pltpu.VMEM((ts, width), BF16)],
        compiler_params=_params(("parallel", "parallel"), "gmlp"),
        name="gmlp",
    )(proj, proj, ln_g.reshape(1, width), ln_b.reshape(1, width), w_s, b_s.T)


def _pair_bcast(mat, q, low_half):
    rows = mat.shape[0]
    a = jnp.broadcast_to(mat[:, 2 * q:2 * q + 1], (rows, LANES))
    b = jnp.broadcast_to(mat[:, 2 * q + 1:2 * q + 2], (rows, LANES))
    return jnp.where(low_half, a, b)


def _ssd_kernel(*refs, width, n_bc, n_cast):
    z_ref, xbc_ref, dt_ref, cw_ref, cb_ref, dtb_ref, alog_ref, dskip_ref, ng_ref = refs[:9]
    o_ref = refs[9 + n_cast]
    xpad_ref, xs_ref, state_ref, y_ref = refs[10 + 2 * n_cast:]
    _side_cast(refs[9:9 + n_cast], refs[10 + n_cast:10 + 2 * n_cast])
    L = SSD_CHUNK
    c = pl.program_id(1)
    pad = 8

    @pl.when(c == 0)
    def _():
        xpad_ref[0:pad, :] = jnp.zeros((pad, xpad_ref.shape[1]), F32)
        state_ref[...] = jnp.zeros_like(state_ref)

    @pl.when(c > 0)
    def _():
        xpad_ref[0:pad, :] = xpad_ref[L:L + pad, :]

    xpad_ref[pad:pad + L, :] = xbc_ref[0].astype(F32)

    conv = cb_ref[...] + cw_ref[0:1, :] * xpad_ref[pad - 3:pad - 3 + L, :]
    for k in range(1, SSD_CONV):
        conv = conv + cw_ref[k:k + 1, :] * xpad_ref[pad - 3 + k:pad - 3 + k + L, :]
    xs_ref[...] = _silu(conv)

    heads = width // SSD_HEAD_DIM
    dt_raw = dt_ref[0]
    dt_raw = dt_raw + pltpu.roll(dt_raw, LANES - heads, axis=1)
    dt = jax.nn.softplus(dt_raw + dtb_ref[...])
    a_dt = dt * (-jnp.exp(alog_ref[...]))
    row = lax.broadcasted_iota(jnp.int32, (L, L), 0)
    col = lax.broadcasted_iota(jnp.int32, (L, L), 1)
    causal = col <= row
    tri = jnp.where(causal, 1.0, 0.0).astype(BF16)
    p0, p1, p2 = _split3_bf16(a_dt)
    a_cum = _dot(tri, p0) + _dot(tri, p1) + _dot(tri, p2)
    a_cum_t = a_cum.T
    a_last = a_cum[L - 1:L, :]
    dt_t = dt.T
    e_cum = jnp.exp(a_cum)
    dt_end = dt * jnp.exp(a_last - a_cum)

    low_half = lax.broadcasted_iota(jnp.int32, (L, LANES), 1) < SSD_HEAD_DIM
    heads_per_group = width // SSD_HEAD_DIM // SSD_GROUPS
    pairs_per_group = heads_per_group // 2
    gw = width // SSD_GROUPS
    for g in range(SSD_GROUPS):
        bm = xs_ref[:, width + g * SSD_STATE:width + (g + 1) * SSD_STATE]
        cm = xs_ref[:, width + n_bc + g * SSD_STATE:width + n_bc + (g + 1) * SSD_STATE]
        bm16 = bm.astype(BF16)
        cm16 = cm.astype(BF16)
        cb = _dot_nt(cm16, bm16)
        y_off_g = _dot(cm16, state_ref[:, g * gw:(g + 1) * gw].astype(BF16))
        bm_t16 = bm.T.astype(BF16)
        for pq in range(pairs_per_group):
            q = g * pairs_per_group + pq
            lanes = slice(q * LANES, (q + 1) * LANES)
            xs_p = xs_ref[:, lanes]
            ms = []
            for r in (2 * q, 2 * q + 1):
                seg = a_cum[:, r:r + 1] - a_cum_t[r:r + 1, :]
                decay = jnp.exp(jnp.where(causal, seg, -jnp.inf))
                ms.append((cb * decay * dt_t[r:r + 1, :]).astype(BF16))
            xs16 = xs_p.astype(BF16)
            zero = jnp.zeros_like(xs16)
            rhs = jnp.concatenate([jnp.where(low_half, xs16, zero), jnp.where(low_half, zero, xs16)], axis=0)
            y_diag = _dot(jnp.concatenate(ms, axis=1), rhs)
            e_p = _pair_bcast(e_cum, q, low_half)
            y_p = y_diag + y_off_g[:, pq * LANES:(pq + 1) * LANES] * e_p + dskip_ref[:, lanes] * xs_p
            y_ref[:, lanes] = y_p
            xdec16 = (xs_p * _pair_bcast(dt_end, q, low_half)).astype(BF16)
            s_new = _dot(bm_t16, xdec16)
            state_ref[:, lanes] = state_ref[:, lanes] * e_p[L - 1:L, :] + s_new

    z = z_ref[0].astype(F32)
    yz = y_ref[...] * _silu(z)
    ms = jnp.mean(yz * yz, axis=-1, keepdims=True)
    o_ref[0] = (yz * lax.rsqrt(ms + 1e-6) * ng_ref[...]).astype(o_ref.dtype)


def _ssd(proj, dt_raw, conv_w, conv_b, dt_bias, a_log, d_skip, norm_g, *, width, z_col, xbc_col, cast=()):
    bsz, s, _ = proj.shape
    L = SSD_CHUNK
    conv_ch = conv_w.shape[1]
    n_bc = SSD_GROUPS * SSD_STATE
    heads = width // SSD_HEAD_DIM
    assert conv_ch == width + 2 * n_bc and z_col % width == 0 and xbc_col % conv_ch == 0

    def padl(v):
        return jnp.pad(v.astype(F32), (0, LANES - heads)).reshape(1, LANES)

    grid = (bsz, s // L)
    cast_arrays, cast_in, cast_out, cast_shapes = _side_cast_plan(cast, grid)
    const = lambda shape: pl.BlockSpec(shape, lambda b, c: (0,) * len(shape))
    return pl.pallas_call(
        functools.partial(_ssd_kernel, width=width, n_bc=n_bc, n_cast=len(cast)),
        out_shape=[jax.ShapeDtypeStruct((bsz, s, width), BF16)] + cast_shapes,
        grid=grid,
        in_specs=[
            pl.BlockSpec((1, L, width), lambda b, c: (b, c, z_col // width)),
            pl.BlockSpec((1, L, conv_ch), lambda b, c: (b, c, xbc_col // conv_ch)),
            pl.BlockSpec((1, L, LANES), lambda b, c: (b, c, 0)),
            const((SSD_CONV, conv_ch)),
            const((1, conv_ch)),
            const((1, LANES)),
            const((1, LANES)),
            const((1, width)),
            const((1, width)),
        ] + cast_in,
        out_specs=[pl.BlockSpec((1, L, width), lambda b, c: (b, c, 0))] + cast_out,
        scratch_shapes=[
            pltpu.VMEM((L + 8, conv_ch), F32),
            pltpu.VMEM((L, conv_ch), F32),
            pltpu.VMEM((SSD_STATE, width), F32),
            pltpu.VMEM((L, width), F32),
        ],
        compiler_params=_params(("parallel", "arbitrary"), "ssd"),
        name="ssd",
    )(proj, proj, dt_raw, conv_w, conv_b.reshape(1, conv_ch), padl(dt_bias), padl(a_log),
      jnp.repeat(d_skip.astype(F32), SSD_HEAD_DIM).reshape(1, width), norm_g.reshape(1, width), *cast_arrays)


def _sb_kernel(*refs, heads, n_cast):
    q_ref, k_ref, v_ref = refs[:3]
    o_ref = refs[3 + n_cast]
    _side_cast(refs[3:3 + n_cast], refs[4 + n_cast:])
    _sb_tile(q_ref, k_ref, v_ref, o_ref, heads=heads)


def _sb_tile(q_ref, k_ref, v_ref, o_ref, *, heads):
    T = SB_TILE
    dh = ATT_HEAD_DIM
    i = pl.program_id(2)
    row = lax.broadcasted_iota(jnp.int32, (T, T), 0)
    col = lax.broadcasted_iota(jnp.int32, (T, T), 1)
    past = col < row
    suffix = jnp.where(row >= col, 1.0, 0.0).astype(BF16)

    def step(j, state, diagonal):
        start = pl.multiple_of(j * T, T)
        cols = [slice(h * dh, (h + 1) * dh) for h in range(heads)]
        logits = [_dot_nt(q_ref[0, :, c], k_ref[0, pl.ds(start, T), c]) for c in cols]
        cost = [jnp.maximum(x, 0.0) + jnp.log2(1.0 + jnp.exp2(-jnp.abs(x))) for x in logits]
        if diagonal:
            cost = [jnp.where(past, c, 0.0) for c in cost]
        rinc = [_dot(c.astype(BF16), suffix) for c in cost]
        w = [jnp.exp2(jnp.minimum(x - r, 0.0) - st[0]) for x, r, st in zip(logits, rinc, state)]
        if diagonal:
            w = [jnp.where(past, wh, 0.0) for wh in w]
        acc = [st[1] + _dot(wh.astype(BF16), v_ref[0, pl.ds(start, T), c]) for wh, st, c in zip(w, state, cols)]
        return tuple((st[0] + r[:, 0:1], a) for st, r, a in zip(state, rinc, acc))

    zero = (jnp.zeros((T, 1), F32), jnp.zeros((T, dh), F32))
    state = step(i, (zero,) * heads, True)

    def body(loop):
        n, _, st = loop
        st = step(i - 1 - n, st, False)
        lowest = functools.reduce(jnp.minimum, [s[0] for s in st])
        return n + 1, jnp.min(lowest), st

    def live(loop):
        n, lowest, _ = loop
        return jnp.logical_and(n < i, lowest < SB_DEAD)

    _, _, state = lax.while_loop(live, body, (jnp.int32(0), jnp.float32(0.0), state))
    for h in range(heads):
        o_ref[0, :, h * dh:(h + 1) * dh] = state[h][1].astype(o_ref.dtype)


def _stick_breaking(proj, *, heads, q_col, k_col, v_col, heads_per_step=6, cast=()):
    bsz, s, _ = proj.shape
    T = SB_TILE
    hb = heads_per_step
    wb = hb * ATT_HEAD_DIM
    assert heads % hb == 0 and q_col % wb == 0 and k_col % wb == 0 and v_col % wb == 0 and s % T == 0
    grid = (bsz, heads // hb, s // T)
    cast_arrays, cast_in, cast_out, cast_shapes = _side_cast_plan(cast, grid)
    return pl.pallas_call(
        functools.partial(_sb_kernel, heads=hb, n_cast=len(cast)),
        out_shape=[jax.ShapeDtypeStruct((bsz, s, heads * ATT_HEAD_DIM), BF16)] + cast_shapes,
        grid=grid,
        in_specs=[
            pl.BlockSpec((1, T, wb), lambda b, h, i: (b, i, q_col // wb + h)),
            pl.BlockSpec((1, s, wb), lambda b, h, i: (b, 0, k_col // wb + h)),
            pl.BlockSpec((1, s, wb), lambda b, h, i: (b, 0, v_col // wb + h)),
        ] + cast_in,
        out_specs=[pl.BlockSpec((1, T, wb), lambda b, h, i: (b, i, h))] + cast_out,
        compiler_params=_params(("arbitrary", "arbitrary", "arbitrary"), "stick_breaking"),
        name="stick_breaking",
    )(proj, proj, proj, *cast_arrays)


def _head_rms(v, g):
    ms = jnp.mean(v * v, axis=-1, keepdims=True)
    return v * lax.rsqrt(ms + 1e-6) * g


def _moba_kernel(q_ref, k_ref, v_ref, qg_ref, kg_ref, o_ref, kn_ref, kmh_ref, kml_ref, *, scale, n_blk):
    BLK = MOBA_BLOCK
    i = pl.program_id(2)

    @pl.when(i == 0)
    def _():
        kmh_ref[...] = jnp.zeros_like(kmh_ref)
        kml_ref[...] = jnp.zeros_like(kml_ref)
        for n in range(n_blk):
            rows = slice(n * BLK, (n + 1) * BLK)
            kn = _head_rms(k_ref[0, rows, :].astype(F32), kg_ref[...])
            kn_ref[rows, :] = kn.astype(BF16)
            hi, lo = _split_bf16(jnp.mean(kn, axis=0, keepdims=True))
            kmh_ref[n:n + 1, :] = hi
            kml_ref[n:n + 1, :] = lo

    qn = _head_rms(q_ref[0].astype(F32), qg_ref[...])
    q16 = (qn * scale).astype(BF16)
    nb_pad = -(-n_blk // 8) * 8
    blk = lax.broadcasted_iota(jnp.int32, (nb_pad, BLK), 0)
    row = lax.broadcasted_iota(jnp.int32, (BLK, BLK), 0)
    col = lax.broadcasted_iota(jnp.int32, (BLK, BLK), 1)

    def q_block(iv):
        rows = lambda n: slice(n * BLK, (n + 1) * BLK)
        scores = [jnp.where(col <= row, _dot_nt(q16, kn_ref[rows(iv), :]), -jnp.inf)]
        if iv > MOBA_TOPK:
            qh, ql = _split_bf16(qn)
            kmh = kmh_ref[...]
            gate = (_dot_nt(kmh, qh) + _dot_nt(kmh, ql) + _dot_nt(kml_ref[...], qh))[0:nb_pad]
            gate = jnp.where(blk < iv, gate, -jnp.inf)
            bias = jnp.zeros((nb_pad, BLK), F32)
            for n in range(iv):
                gn = gate[n:n + 1, :]
                ahead = (gate > gn) | ((gate == gn) & (blk < n))
                rank = jnp.sum(jnp.where(ahead, 1.0, 0.0), axis=0, keepdims=True)
                bias = jnp.where((blk == n) & (rank >= MOBA_TOPK), -jnp.inf, bias)
            bias = jnp.concatenate([bias, jnp.zeros((LANES - nb_pad, BLK), F32)], axis=0).T
        for n in range(iv):
            s = _dot_nt(q16, kn_ref[rows(n), :])
            if iv > MOBA_TOPK:
                s = s + bias[:, n:n + 1]
            scores.append(s)
        m = functools.reduce(jnp.maximum, [jnp.max(s, axis=-1, keepdims=True) for s in scores])
        l = 0.0
        acc = 0.0
        for s, n in zip(scores, [iv] + list(range(iv))):
            p = jnp.exp(s - m)
            l = l + jnp.sum(p, axis=-1, keepdims=True)
            acc = acc + _dot(p.astype(BF16), v_ref[0, rows(n), :])
        o_ref[0] = (acc / l).astype(o_ref.dtype)

    for iv in range(n_blk):
        pl.when(i == iv)(functools.partial(q_block, iv))


def _moba(proj, q_norm_g, k_norm_g, *, heads, q_col, k_col, v_col):
    bsz, s, _ = proj.shape
    dh = ATT_HEAD_DIM
    BLK = MOBA_BLOCK
    n_blk = s // BLK
    assert s % BLK == 0 and n_blk <= LANES
    return pl.pallas_call(
        functools.partial(_moba_kernel, scale=dh ** -0.5, n_blk=n_blk),
        out_shape=jax.ShapeDtypeStruct((bsz, s, heads * dh), BF16),
        grid=(bsz, heads, n_blk),
        in_specs=[
            pl.BlockSpec((1, BLK, dh), lambda b, h, i: (b, i, q_col // dh + h)),
            pl.BlockSpec((1, s, dh), lambda b, h, i: (b, 0, k_col // dh + h)),
            pl.BlockSpec((1, s, dh), lambda b, h, i: (b, 0, v_col // dh + h)),
            pl.BlockSpec((1, dh), lambda b, h, i: (0, 0)),
            pl.BlockSpec((1, dh), lambda b, h, i: (0, 0)),
        ],
        out_specs=pl.BlockSpec((1, BLK, dh), lambda b, h, i: (b, i, h)),
        scratch_shapes=[
            pltpu.VMEM((s, dh), BF16),
            pltpu.VMEM((LANES, dh), BF16),
            pltpu.VMEM((LANES, dh), BF16),
        ],
        compiler_params=_params(("parallel", "parallel", "arbitrary"), "moba"),
        name="moba",
    )(proj, proj, proj, q_norm_g.reshape(1, dh), k_norm_g.reshape(1, dh))


def kernel(x, c, norm_mix_g, norm_ffn_g, ada_w, ada_b, ffn_w_gate, ffn_w_up, ffn_w_down, ab_w_in, ab_w_out, gm_ln_g, gm_ln_b, gm_w_s, gm_b_s, ssd_conv_w, ssd_conv_b, ssd_dt_bias, ssd_a_log, ssd_d, ssd_norm_g, cd_w_in, cd_w_out, moba_q_norm_g, moba_k_norm_g):
    bsz, s, d = x.shape
    depth = ada_w.shape[0]
    mod = _ada_mod(c, ada_w, ada_b)

    a_width = gm_ln_g.shape[1]
    b_width = ssd_norm_g.shape[1]
    conv_ch = ssd_conv_w.shape[2]
    ssd_heads = ssd_dt_bias.shape[1]
    ab_main = 2 * a_width + b_width + conv_ch
    n_att = cd_w_out.shape[1] // ATT_HEAD_DIM
    d_heads = n_att // 4
    c_heads = n_att - d_heads
    c_width = c_heads * ATT_HEAD_DIM
    d_width = d_heads * ATT_HEAD_DIM

    ffn_w = (ffn_w_gate, ffn_w_up, ffn_w_down)
    for layer in range(depth):
        sh_m, sc_m, g_m, sh_f, sc_f, g_f = [mod[layer, :, None, k * d:(k + 1) * d] for k in range(6)]
        if layer % 2 == 0:
            e = layer // 2
            w_in_t = ab_w_in[e].T
            proj, dt_raw = _in_proj(x, norm_mix_g[layer], sc_m, sh_m, w_in_t.astype(BF16), n_cols=ab_main,
                                    n_act_cols=2 * a_width, w_extra=w_in_t[ab_main:], w_transposed=True)
            y_a = _gmlp(proj, gm_ln_g[e], gm_ln_b[e], gm_w_s[e], gm_b_s[e], width=a_width)
            y_b, w_out16, *ffn16 = _ssd(
                proj, dt_raw, ssd_conv_w[e], ssd_conv_b[e], ssd_dt_bias[e], ssd_a_log[e], ssd_d[e], ssd_norm_g[e],
                width=b_width, z_col=2 * a_width, xbc_col=2 * a_width + b_width,
                cast=[(ab_w_out, e)] + [(w, layer) for w in ffn_w])
            x = _out_proj(y_a, y_b, w_out16[0], x, g_m)
        else:
            o = layer // 2
            col_scale = jnp.where(jnp.arange(cd_w_in.shape[2]) < c_width, ATT_HEAD_DIM ** -0.5 * LOG2E, 1.0)
            proj = _in_proj(x, norm_mix_g[layer], sc_m, sh_m, (cd_w_in[o] * col_scale).astype(BF16))
            y_c, w_out16, *ffn16 = _stick_breaking(
                proj, heads=c_heads, q_col=0, k_col=c_width, v_col=2 * c_width,
                cast=[(cd_w_out, o)] + [(w, layer) for w in ffn_w])
            y_d = _moba(proj, moba_q_norm_g[o], moba_k_norm_g[o], heads=d_heads,
                        q_col=3 * c_width, k_col=3 * c_width + d_width, v_col=3 * c_width + 2 * d_width)
            x = _out_proj(y_c, y_d, w_out16[0], x, g_m, tn=1024)
        x = _ffn(x, norm_ffn_g[layer], sc_f, sh_f, g_f, *ffn16, 0)
    return x
```

```python
import functools

import jax
import jax.numpy as jnp
from jax import lax
from jax.experimental import pallas as pl
from jax.experimental.pallas import tpu as pltpu

F32 = jnp.float32
BF16 = jnp.bfloat16

LANES = 128
A_CHUNK = 128
A_GROUP = 128
SSD_HEAD_DIM = 64
SSD_GROUPS = 4
SSD_STATE = 128
SSD_CONV = 4
SSD_CHUNK = 128
ATT_HEAD_DIM = 128
MOBA_BLOCK = 256
MOBA_TOPK = 3
SB_TILE = 256
SB_DEAD = 160.0
NORM_ROWS = 128
LOG2E = 1.4426950408889634
MIB = 1 << 20
V7X_VMEM_MIB = 64
VMEM_LIMIT_MIB = {"ada_mod": 40, "in_proj": 52, "out_proj": 48, "ffn": 48, "gmlp": 40, "ssd": 40,
                  "stick_breaking": 40, "moba": 32}
assert max(VMEM_LIMIT_MIB.values()) < V7X_VMEM_MIB


def _params(semantics, name):
    return pltpu.CompilerParams(dimension_semantics=semantics, vmem_limit_bytes=VMEM_LIMIT_MIB[name] * MIB)


def _split_bf16(v):
    hi = v.astype(BF16)
    lo = (v - hi.astype(F32)).astype(BF16)
    return hi, lo


def _split3_bf16(v):
    hi = v.astype(BF16)
    r = v - hi.astype(F32)
    mid = r.astype(BF16)
    lo = (r - mid.astype(F32)).astype(BF16)
    return hi, mid, lo


def _dot(a, b):
    return jnp.dot(a, b, preferred_element_type=F32)


def _dot_nt(a, b):
    return lax.dot_general(a, b, (((1,), (1,)), ((), ())), preferred_element_type=F32)


def _silu(v):
    return v * (0.5 * jnp.tanh(0.5 * v) + 0.5)


def _side_cast_plan(items, grid):
    n_steps = 1
    for g in grid:
        n_steps *= g

    def linear_step(*ids):
        t = ids[0]
        for g, i in zip(grid[1:], ids[1:]):
            t = t * g + i
        return t

    in_specs, out_specs, out_shapes = [], [], []
    for a, layer in items:
        _, rows, cols = a.shape
        rb = next(r for r in range(16, rows + 1, 16) if rows % r == 0 and rows // r <= n_steps)
        last = rows // rb - 1
        in_specs.append(pl.BlockSpec(
            (1, rb, cols), lambda *ids, last=last, layer=layer: (layer, jnp.minimum(linear_step(*ids), last), 0)))
        out_specs.append(pl.BlockSpec(
            (1, rb, cols), lambda *ids, last=last: (0, jnp.minimum(linear_step(*ids), last), 0)))
        out_shapes.append(jax.ShapeDtypeStruct((1, rows, cols), BF16))
    return [a for a, _ in items], in_specs, out_specs, out_shapes


def _side_cast(in_refs, out_refs):
    for i_ref, o_ref in zip(in_refs, out_refs):
        o_ref[...] = i_ref[...].astype(BF16)


def _ada_kernel(c_ref, w_ref, b_ref, o_ref):
    ca = _silu(c_ref[...])
    o_ref[0] = _dot(ca.astype(BF16), w_ref[0].astype(BF16)) + b_ref[0]


def _ada_mod(c, ada_w, ada_b):
    depth, d, n = ada_w.shape
    bsz = c.shape[0]
    tn = 1024
    return pl.pallas_call(
        _ada_kernel,
        out_shape=jax.ShapeDtypeStruct((depth, bsz, n), F32),
        grid=(depth, n // tn),
        in_specs=[
            pl.BlockSpec((bsz, d), lambda l, j: (0, 0)),
            pl.BlockSpec((1, d, tn), lambda l, j: (l, 0, j)),
            pl.BlockSpec((1, 1, tn), lambda l, j: (l, 0, j)),
        ],
        out_specs=pl.BlockSpec((1, bsz, tn), lambda l, j: (l, 0, j)),
        compiler_params=_params(("arbitrary", "arbitrary"), "ada_mod"),
        name="ada_mod",
    )(c, ada_w, ada_b.reshape(depth, 1, n))


def _norm_chunk(x_ref, scale, shift, h_ref, chunk):
    rows = pl.ds(pl.multiple_of(chunk * NORM_ROWS, NORM_ROWS), NORM_ROWS)
    x = x_ref[0, rows, :]
    rs = lax.rsqrt(jnp.mean(x * x, axis=-1, keepdims=True) + 1e-6)
    h_ref[rows, :] = (x * rs * scale + shift).astype(BF16)


def _norm_mod_store(x_ref, g_ref, sc_ref, sh_ref, h_ref):
    scale = g_ref[...] * (1.0 + sc_ref[0])
    shift = sh_ref[0]

    def body(r, carry):
        _norm_chunk(x_ref, scale, shift, h_ref, r)
        return carry

    lax.fori_loop(0, h_ref.shape[0] // NORM_ROWS, body, 0)


def _norm_ahead(x_ref, g_ref, sc_ref, sh_ref, h_ref, j, n_j):
    n_chunks = h_ref.shape[0] // NORM_ROWS
    per_step = -(-n_chunks // (n_j - 1))
    scale = g_ref[...] * (1.0 + sc_ref[0])
    shift = sh_ref[0]
    for k in range(per_step):
        _norm_chunk(x_ref, scale, shift, h_ref, jnp.clip((j - 1) * per_step + k, 0, n_chunks - 1))


def _ahead_maps(n_b, n_i):
    def tile(b, i, j):
        r = b * n_i + i
        t = jnp.where(jnp.logical_and(r == 0, j == 0), 0, jnp.minimum(r + 1, n_b * n_i - 1))
        return t // n_i, t % n_i

    def x_map(b, i, j):
        tb, ti = tile(b, i, j)
        return tb, ti, 0

    def vec_map(b, i, j):
        return tile(b, i, j)[0], 0, 0

    return x_map, vec_map


def _inproj_kernel(*refs, n_j, n_act_tiles, has_extra, w_transposed):
    if has_extra:
        x_ref, g_ref, sc_ref, sh_ref, w_ref, wx_ref, o_ref, ox_ref, hm_a, hm_b = refs
    else:
        x_ref, g_ref, sc_ref, sh_ref, w_ref, o_ref, hm_a, hm_b = refs
    j = pl.program_id(2)
    r = pl.program_id(0) * pl.num_programs(1) + pl.program_id(1)
    mm = _dot_nt if w_transposed else _dot

    @pl.when(jnp.logical_and(r == 0, j == 0))
    def _():
        _norm_mod_store(x_ref, g_ref, sc_ref, sh_ref, hm_a)

    def tile_step(cur, nxt, act):
        def run():
            if has_extra:
                @pl.when(j == 0)
                def _():
                    ox_ref[0] = mm(cur[...], wx_ref[...])
            _norm_ahead(x_ref, g_ref, sc_ref, sh_ref, nxt, j, n_j)
            acc = mm(cur[...], w_ref[...])
            o_ref[0] = (jax.nn.gelu(acc) if act else acc).astype(o_ref.dtype)
        return run

    for parity, (cur, nxt) in enumerate(((hm_a, hm_b), (hm_b, hm_a))):
        mine = r % 2 == parity
        if n_act_tiles:
            pl.when(jnp.logical_and(mine, j < n_act_tiles))(tile_step(cur, nxt, True))
            pl.when(jnp.logical_and(mine, j >= n_act_tiles))(tile_step(cur, nxt, False))
        else:
            pl.when(mine)(tile_step(cur, nxt, False))


def _in_proj(x, g, sc, sh, w, *, n_cols=None, n_act_cols=0, w_extra=None, w_transposed=False, tm=1024, tn=1024):
    bsz, s, d = x.shape
    n = (w.shape[0] if w_transposed else w.shape[1]) if n_cols is None else n_cols
    tm = min(tm, s)
    assert n % tn == 0 and n_act_cols % tn == 0 and tm % NORM_ROWS == 0 and n // tn >= 2
    has_extra = w_extra is not None
    x_map, vec_map = _ahead_maps(bsz, s // tm)
    in_specs = [
        pl.BlockSpec((1, tm, d), x_map),
        pl.BlockSpec((1, d), lambda b, i, j: (0, 0)),
        pl.BlockSpec((1, 1, d), vec_map),
        pl.BlockSpec((1, 1, d), vec_map),
        pl.BlockSpec((tn, d), lambda b, i, j: (j, 0)) if w_transposed else pl.BlockSpec((d, tn), lambda b, i, j: (0, j)),
    ]
    args = [x, g.reshape(1, d), sc, sh, w]
    out_shape = [jax.ShapeDtypeStruct((bsz, s, n), BF16)]
    out_specs = [pl.BlockSpec((1, tm, tn), lambda b, i, j: (b, i, j))]
    if has_extra:
        axis = 0 if w_transposed else 1
        ne = w_extra.shape[axis]
        assert 2 * ne <= LANES
        pad = [(0, 0), (0, 0)]
        pad[axis] = (0, LANES - 2 * ne)
        wx = jnp.pad(jnp.concatenate(_split_bf16(w_extra), axis=axis), pad)
        in_specs.append(pl.BlockSpec(wx.shape, lambda b, i, j: (0, 0)))
        args.append(wx)
        out_shape.append(jax.ShapeDtypeStruct((bsz, s, LANES), F32))
        out_specs.append(pl.BlockSpec((1, tm, LANES), lambda b, i, j: (b, i, 0)))
    res = pl.pallas_call(
        functools.partial(_inproj_kernel, n_j=n // tn, n_act_tiles=n_act_cols // tn, has_extra=has_extra,
                          w_transposed=w_transposed),
        out_shape=out_shape,
        grid=(bsz, s // tm, n // tn),
        in_specs=in_specs,
        out_specs=out_specs,
        scratch_shapes=[pltpu.VMEM((tm, d), BF16)] * 2,
        compiler_params=_params(("arbitrary", "arbitrary", "arbitrary"), "in_proj"),
        name="in_proj",
    )(*args)
    return res if has_extra else res[0]


def _outproj_kernel(ya_ref, yb_ref, wa_ref, wb_ref, x_ref, gate_ref, o_ref):
    acc = _dot(ya_ref[0], wa_ref[...]) + _dot(yb_ref[0], wb_ref[...])
    o_ref[0] = x_ref[0] + (1.0 + gate_ref[0]) * acc


def _out_proj(ya, yb, w, x, gate, *, tm=1024, tn=512):
    bsz, s, d = x.shape
    ka, kb = ya.shape[2], yb.shape[2]
    tm = min(tm, s)
    assert w.shape[0] == ka + kb and ka % kb == 0
    return pl.pallas_call(
        _outproj_kernel,
        out_shape=jax.ShapeDtypeStruct((bsz, s, d), F32),
        grid=(bsz, s // tm, d // tn),
        in_specs=[
            pl.BlockSpec((1, tm, ka), lambda b, i, j: (b, i, 0)),
            pl.BlockSpec((1, tm, kb), lambda b, i, j: (b, i, 0)),
            pl.BlockSpec((ka, tn), lambda b, i, j: (0, j)),
            pl.BlockSpec((kb, tn), lambda b, i, j: (ka // kb, j)),
            pl.BlockSpec((1, tm, tn), lambda b, i, j: (b, i, j)),
            pl.BlockSpec((1, 1, tn), lambda b, i, j: (b, 0, j)),
        ],
        out_specs=pl.BlockSpec((1, tm, tn), lambda b, i, j: (b, i, j)),
        compiler_params=_params(("parallel", "parallel", "arbitrary"), "out_proj"),
        name="out_proj",
    )(ya, yb, w, w, x, gate)


def _ffn_kernel(x_ref, g_ref, sc_ref, sh_ref, gate_ref, wg_ref, wu_ref, wd_ref, o_ref, hf_ref):
    f = pl.program_id(2)

    @pl.when(f == 0)
    def _():
        _norm_mod_store(x_ref, g_ref, sc_ref, sh_ref, hf_ref)
        o_ref[0] = jnp.zeros(o_ref.shape[1:], F32)

    hf = hf_ref[...]
    a = _silu(_dot(hf, wg_ref[0])) * _dot(hf, wu_ref[0])
    o_ref[0] += _dot(a.astype(BF16), wd_ref[0])

    @pl.when(f == pl.num_programs(2) - 1)
    def _():
        o_ref[0] = x_ref[0] + (1.0 + gate_ref[0]) * o_ref[0]


def _ffn(x, g, sc, sh, gate, wg, wu, wd, layer, *, tm=512, tf=512):
    bsz, s, d = x.shape
    dff = wg.shape[2]
    tm = min(tm, s)
    assert dff % tf == 0 and tm % NORM_ROWS == 0
    vec = pl.BlockSpec((1, 1, d), lambda b, i, f: (b, 0, 0))
    return pl.pallas_call(
        _ffn_kernel,
        out_shape=jax.ShapeDtypeStruct((bsz, s, d), F32),
        grid=(bsz, s // tm, dff // tf),
        in_specs=[
            pl.BlockSpec((1, tm, d), lambda b, i, f: (b, i, 0)),
            pl.BlockSpec((1, d), lambda b, i, f: (0, 0)),
            vec, vec, vec,
            pl.BlockSpec((1, d, tf), lambda b, i, f: (layer, 0, f)),
            pl.BlockSpec((1, d, tf), lambda b, i, f: (layer, 0, f)),
            pl.BlockSpec((1, tf, d), lambda b, i, f: (layer, f, 0)),
        ],
        out_specs=pl.BlockSpec((1, tm, d), lambda b, i, f: (b, i, 0)),
        scratch_shapes=[pltpu.VMEM((tm, d), BF16)],
        compiler_params=_params(("parallel", "parallel", "arbitrary"), "ffn"),
        name="ffn",
    )(x, g.reshape(1, d), sc, sh, gate, wg, wu, wd)


def _gmlp_kernel(u_ref, v_ref, lng_ref, lnb_ref, ws_ref, bst_ref, o_ref, vln_ref):
    ts, width = v_ref.shape[1], v_ref.shape[2]
    v = v_ref[0].astype(F32)
    mu = jnp.mean(v, axis=-1, keepdims=True)
    vc = v - mu
    var = jnp.mean(vc * vc, axis=-1, keepdims=True)
    vln_ref[...] = (vc * lax.rsqrt(var + 1e-5) * lng_ref[...] + lnb_ref[...]).astype(BF16)
    row = lax.broadcasted_iota(jnp.int32, (A_CHUNK, A_CHUNK), 0)
    col = lax.broadcasted_iota(jnp.int32, (A_CHUNK, A_CHUNK), 1)
    causal = col <= row
    for h in range(width // A_GROUP):
        w = jnp.where(causal, ws_ref[h], 0.0).astype(BF16)
        bias = bst_ref[:, h:h + 1]
        cols = slice(h * A_GROUP, (h + 1) * A_GROUP)
        for c in range(ts // A_CHUNK):
            rows = slice(c * A_CHUNK, (c + 1) * A_CHUNK)
            mixed = _dot(w, vln_ref[rows, cols]) + bias
            o_ref[0, rows, cols] = (u_ref[0, rows, cols].astype(F32) * mixed).astype(o_ref.dtype)


def _gmlp(proj, ln_g, ln_b, w_s, b_s, *, width, ts=1024):
    bsz, s, _ = proj.shape
    heads = w_s.shape[0]
    ts = min(ts, s)
    return pl.pallas_call(
        _gmlp_kernel,
        out_shape=jax.ShapeDtypeStruct((bsz, s, width), BF16),
        grid=(bsz, s // ts),
        in_specs=[
            pl.BlockSpec((1, ts, width), lambda b, i: (b, i, 0)),
            pl.BlockSpec((1, ts, width), lambda b, i: (b, i, 1)),
            pl.BlockSpec((1, width), lambda b, i: (0, 0)),
            pl.BlockSpec((1, width), lambda b, i: (0, 0)),
            pl.BlockSpec((heads, A_CHUNK, A_CHUNK), lambda b, i: (0, 0, 0)),
            pl.BlockSpec((A_CHUNK, heads), lambda b, i: (0, 0)),
        ],
        out_specs=pl.BlockSpec((1, ts, width), lambda b, i: (b, i, 0)),
        scratch_shapes=[pltpu.VMEM((ts, width), BF16)],
        compiler_params=_params(("parallel", "parallel"), "gmlp"),
        name="gmlp",
    )(proj, proj, ln_g.reshape(1, width), ln_b.reshape(1, width), w_s, b_s.T)


def _pair_bcast(mat, q, low_half):
    rows = mat.shape[0]
    a = jnp.broadcast_to(mat[:, 2 * q:2 * q + 1], (rows, LANES))
    b = jnp.broadcast_to(mat[:, 2 * q + 1:2 * q + 2], (rows, LANES))
    return jnp.where(low_half, a, b)


def _ssd_kernel(*refs, width, n_bc, n_cast):
    z_ref, xbc_ref, dt_ref, cw_ref, cb_ref, dtb_ref, alog_ref, dskip_ref, ng_ref = refs[:9]
    o_ref = refs[9 + n_cast]
    xpad_ref, xs_ref, state_ref, y_ref = refs[10 + 2 * n_cast:]
    _side_cast(refs[9:9 + n_cast], refs[10 + n_cast:10 + 2 * n_cast])
    L = SSD_CHUNK
    c = pl.program_id(1)
    pad = 8

    @pl.when(c == 0)
    def _():
        xpad_ref[0:pad, :] = jnp.zeros((pad, xpad_ref.shape[1]), F32)
        state_ref[...] = jnp.zeros_like(state_ref)

    @pl.when(c > 0)
    def _():
        xpad_ref[0:pad, :] = xpad_ref[L:L + pad, :]

    xpad_ref[pad:pad + L, :] = xbc_ref[0].astype(F32)

    conv = cb_ref[...] + cw_ref[0:1, :] * xpad_ref[pad - 3:pad - 3 + L, :]
    for k in range(1, SSD_CONV):
        conv = conv + cw_ref[k:k + 1, :] * xpad_ref[pad - 3 + k:pad - 3 + k + L, :]
    xs_ref[...] = _silu(conv)

    heads = width // SSD_HEAD_DIM
    dt_raw = dt_ref[0]
    dt_raw = dt_raw + pltpu.roll(dt_raw, LANES - heads, axis=1)
    dt = jax.nn.softplus(dt_raw + dtb_ref[...])
    a_dt = dt * (-jnp.exp(alog_ref[...]))
    row = lax.broadcasted_iota(jnp.int32, (L, L), 0)
    col = lax.broadcasted_iota(jnp.int32, (L, L), 1)
    causal = col <= row
    tri = jnp.where(causal, 1.0, 0.0).astype(BF16)
    p0, p1, p2 = _split3_bf16(a_dt)
    a_cum = _dot(tri, p0) + _dot(tri, p1) + _dot(tri, p2)
    a_cum_t = a_cum.T
    a_last = a_cum[L - 1:L, :]
    dt_t = dt.T
    e_cum = jnp.exp(a_cum)
    dt_end = dt * jnp.exp(a_last - a_cum)

    low_half = lax.broadcasted_iota(jnp.int32, (L, LANES), 1) < SSD_HEAD_DIM
    heads_per_group = width // SSD_HEAD_DIM // SSD_GROUPS
    pairs_per_group = heads_per_group // 2
    gw = width // SSD_GROUPS
    for g in range(SSD_GROUPS):
        bm = xs_ref[:, width + g * SSD_STATE:width + (g + 1) * SSD_STATE]
        cm = xs_ref[:, width + n_bc + g * SSD_STATE:width + n_bc + (g + 1) * SSD_STATE]
        bm16 = bm.astype(BF16)
        cm16 = cm.astype(BF16)
        cb = _dot_nt(cm16, bm16)
        y_off_g = _dot(cm16, state_ref[:, g * gw:(g + 1) * gw].astype(BF16))
        bm_t16 = bm.T.astype(BF16)
        for pq in range(pairs_per_group):
            q = g * pairs_per_group + pq
            lanes = slice(q * LANES, (q + 1) * LANES)
            xs_p = xs_ref[:, lanes]
            ms = []
            for r in (2 * q, 2 * q + 1):
                seg = a_cum[:, r:r + 1] - a_cum_t[r:r + 1, :]
                decay = jnp.exp(jnp.where(causal, seg, -jnp.inf))
                ms.append((cb * decay * dt_t[r:r + 1, :]).astype(BF16))
            xs16 = xs_p.astype(BF16)
            zero = jnp.zeros_like(xs16)
            rhs = jnp.concatenate([jnp.where(low_half, xs16, zero), jnp.where(low_half, zero, xs16)], axis=0)
            y_diag = _dot(jnp.concatenate(ms, axis=1), rhs)
            e_p = _pair_bcast(e_cum, q, low_half)
            y_p = y_diag + y_off_g[:, pq * LANES:(pq + 1) * LANES] * e_p + dskip_ref[:, lanes] * xs_p
            y_ref[:, lanes] = y_p
            xdec16 = (xs_p * _pair_bcast(dt_end, q, low_half)).astype(BF16)
            s_new = _dot(bm_t16, xdec16)
            state_ref[:, lanes] = state_ref[:, lanes] * e_p[L - 1:L, :] + s_new

    z = z_ref[0].astype(F32)
    yz = y_ref[...] * _silu(z)
    ms = jnp.mean(yz * yz, axis=-1, keepdims=True)
    o_ref[0] = (yz * lax.rsqrt(ms + 1e-6) * ng_ref[...]).astype(o_ref.dtype)


def _ssd(proj, dt_raw, conv_w, conv_b, dt_bias, a_log, d_skip, norm_g, *, width, z_col, xbc_col, cast=()):
    bsz, s, _ = proj.shape
    L = SSD_CHUNK
    conv_ch = conv_w.shape[1]
    n_bc = SSD_GROUPS * SSD_STATE
    heads = width // SSD_HEAD_DIM
    assert conv_ch == width + 2 * n_bc and z_col % width == 0 and xbc_col % conv_ch == 0

    def padl(v):
        return jnp.pad(v.astype(F32), (0, LANES - heads)).reshape(1, LANES)

    grid = (bsz, s // L)
    cast_arrays, cast_in, cast_out, cast_shapes = _side_cast_plan(cast, grid)
    const = lambda shape: pl.BlockSpec(shape, lambda b, c: (0,) * len(shape))
    return pl.pallas_call(
        functools.partial(_ssd_kernel, width=width, n_bc=n_bc, n_cast=len(cast)),
        out_shape=[jax.ShapeDtypeStruct((bsz, s, width), BF16)] + cast_shapes,
        grid=grid,
        in_specs=[
            pl.BlockSpec((1, L, width), lambda b, c: (b, c, z_col // width)),
            pl.BlockSpec((1, L, conv_ch), lambda b, c: (b, c, xbc_col // conv_ch)),
            pl.BlockSpec((1, L, LANES), lambda b, c: (b, c, 0)),
            const((SSD_CONV, conv_ch)),
            const((1, conv_ch)),
            const((1, LANES)),
            const((1, LANES)),
            const((1, width)),
            const((1, width)),
        ] + cast_in,
        out_specs=[pl.BlockSpec((1, L, width), lambda b, c: (b, c, 0))] + cast_out,
        scratch_shapes=[
            pltpu.VMEM((L + 8, conv_ch), F32),
            pltpu.VMEM((L, conv_ch), F32),
            pltpu.VMEM((SSD_STATE, width), F32),
            pltpu.VMEM((L, width), F32),
        ],
        compiler_params=_params(("parallel", "arbitrary"), "ssd"),
        name="ssd",
    )(proj, proj, dt_raw, conv_w, conv_b.reshape(1, conv_ch), padl(dt_bias), padl(a_log),
      jnp.repeat(d_skip.astype(F32), SSD_HEAD_DIM).reshape(1, width), norm_g.reshape(1, width), *cast_arrays)


def _sb_kernel(*refs, heads, n_cast):
    q_ref, k_ref, v_ref = refs[:3]
    o_ref = refs[3 + n_cast]
    _side_cast(refs[3:3 + n_cast], refs[4 + n_cast:])
    _sb_tile(q_ref, k_ref, v_ref, o_ref, heads=heads)


def _sb_tile(q_ref, k_ref, v_ref, o_ref, *, heads):
    T = SB_TILE
    dh = ATT_HEAD_DIM
    i = pl.program_id(2)
    row = lax.broadcasted_iota(jnp.int32, (T, T), 0)
    col = lax.broadcasted_iota(jnp.int32, (T, T), 1)
    past = col < row
    suffix = jnp.where(row >= col, 1.0, 0.0).astype(BF16)

    def step(j, state, diagonal):
        start = pl.multiple_of(j * T, T)
        cols = [slice(h * dh, (h + 1) * dh) for h in range(heads)]
        logits = [_dot_nt(q_ref[0, :, c], k_ref[0, pl.ds(start, T), c]) for c in cols]
        cost = [jnp.maximum(x, 0.0) + jnp.log2(1.0 + jnp.exp2(-jnp.abs(x))) for x in logits]
        if diagonal:
            cost = [jnp.where(past, c, 0.0) for c in cost]
        rinc = [_dot(c.astype(BF16), suffix) for c in cost]
        w = [jnp.exp2(jnp.minimum(x - r, 0.0) - st[0]) for x, r, st in zip(logits, rinc, state)]
        if diagonal:
            w = [jnp.where(past, wh, 0.0) for wh in w]
        acc = [st[1] + _dot(wh.astype(BF16), v_ref[0, pl.ds(start, T), c]) for wh, st, c in zip(w, state, cols)]
        return tuple((st[0] + r[:, 0:1], a) for st, r, a in zip(state, rinc, acc))

    zero = (jnp.zeros((T, 1), F32), jnp.zeros((T, dh), F32))
    state = step(i, (zero,) * heads, True)

    def body(loop):
        n, _, st = loop
        st = step(i - 1 - n, st, False)
        lowest = functools.reduce(jnp.minimum, [s[0] for s in st])
        return n + 1, jnp.min(lowest), st

    def live(loop):
        n, lowest, _ = loop
        return jnp.logical_and(n < i, lowest < SB_DEAD)

    _, _, state = lax.while_loop(live, body, (jnp.int32(0), jnp.float32(0.0), state))
    for h in range(heads):
        o_ref[0, :, h * dh:(h + 1) * dh] = state[h][1].astype(o_ref.dtype)


def _stick_breaking(proj, *, heads, q_col, k_col, v_col, heads_per_step=6, cast=()):
    bsz, s, _ = proj.shape
    T = SB_TILE
    hb = heads_per_step
    wb = hb * ATT_HEAD_DIM
    assert heads % hb == 0 and q_col % wb == 0 and k_col % wb == 0 and v_col % wb == 0 and s % T == 0
    grid = (bsz, heads // hb, s // T)
    cast_arrays, cast_in, cast_out, cast_shapes = _side_cast_plan(cast, grid)
    return pl.pallas_call(
        functools.partial(_sb_kernel, heads=hb, n_cast=len(cast)),
        out_shape=[jax.ShapeDtypeStruct((bsz, s, heads * ATT_HEAD_DIM), BF16)] + cast_shapes,
        grid=grid,
        in_specs=[
            pl.BlockSpec((1, T, wb), lambda b, h, i: (b, i, q_col // wb + h)),
            pl.BlockSpec((1, s, wb), lambda b, h, i: (b, 0, k_col // wb + h)),
            pl.BlockSpec((1, s, wb), lambda b, h, i: (b, 0, v_col // wb + h)),
        ] + cast_in,
        out_specs=[pl.BlockSpec((1, T, wb), lambda b, h, i: (b, i, h))] + cast_out,
        compiler_params=_params(("arbitrary", "arbitrary", "arbitrary"), "stick_breaking"),
        name="stick_breaking",
    )(proj, proj, proj, *cast_arrays)


def _head_rms(v, g):
    ms = jnp.mean(v * v, axis=-1, keepdims=True)
    return v * lax.rsqrt(ms + 1e-6) * g


def _moba_kernel(q_ref, k_ref, v_ref, qg_ref, kg_ref, o_ref, kn_ref, kmh_ref, kml_ref, *, scale, n_blk):
    BLK = MOBA_BLOCK
    i = pl.program_id(2)

    @pl.when(i == 0)
    def _():
        kmh_ref[...] = jnp.zeros_like(kmh_ref)
        kml_ref[...] = jnp.zeros_like(kml_ref)
        for n in range(n_blk):
            rows = slice(n * BLK, (n + 1) * BLK)
            kn = _head_rms(k_ref[0, rows, :].astype(F32), kg_ref[...])
            kn_ref[rows, :] = kn.astype(BF16)
            hi, lo = _split_bf16(jnp.mean(kn, axis=0, keepdims=True))
            kmh_ref[n:n + 1, :] = hi
            kml_ref[n:n + 1, :] = lo

    qn = _head_rms(q_ref[0].astype(F32), qg_ref[...])
    q16 = (qn * scale).astype(BF16)
    nb_pad = -(-n_blk // 8) * 8
    blk = lax.broadcasted_iota(jnp.int32, (nb_pad, BLK), 0)
    row = lax.broadcasted_iota(jnp.int32, (BLK, BLK), 0)
    col = lax.broadcasted_iota(jnp.int32, (BLK, BLK), 1)

    def q_block(iv):
        rows = lambda n: slice(n * BLK, (n + 1) * BLK)
        scores = [jnp.where(col <= row, _dot_nt(q16, kn_ref[rows(iv), :]), -jnp.inf)]
        if iv > MOBA_TOPK:
            qh, ql = _split_bf16(qn)
            kmh = kmh_ref[...]
            gate = (_dot_nt(kmh, qh) + _dot_nt(kmh, ql) + _dot_nt(kml_ref[...], qh))[0:nb_pad]
            gate = jnp.where(blk < iv, gate, -jnp.inf)
            bias = jnp.zeros((nb_pad, BLK), F32)
            for n in range(iv):
                gn = gate[n:n + 1, :]
                ahead = (gate > gn) | ((gate == gn) & (blk < n))
                rank = jnp.sum(jnp.where(ahead, 1.0, 0.0), axis=0, keepdims=True)
                bias = jnp.where((blk == n) & (rank >= MOBA_TOPK), -jnp.inf, bias)
            bias = jnp.concatenate([bias, jnp.zeros((LANES - nb_pad, BLK), F32)], axis=0).T
        for n in range(iv):
            s = _dot_nt(q16, kn_ref[rows(n), :])
            if iv > MOBA_TOPK:
                s = s + bias[:, n:n + 1]
            scores.append(s)
        m = functools.reduce(jnp.maximum, [jnp.max(s, axis=-1, keepdims=True) for s in scores])
        l = 0.0
        acc = 0.0
        for s, n in zip(scores, [iv] + list(range(iv))):
            p = jnp.exp(s - m)
            l = l + jnp.sum(p, axis=-1, keepdims=True)
            acc = acc + _dot(p.astype(BF16), v_ref[0, rows(n), :])
        o_ref[0] = (acc / l).astype(o_ref.dtype)

    for iv in range(n_blk):
        pl.when(i == iv)(functools.partial(q_block, iv))


def _moba(proj, q_norm_g, k_norm_g, *, heads, q_col, k_col, v_col):
    bsz, s, _ = proj.shape
    dh = ATT_HEAD_DIM
    BLK = MOBA_BLOCK
    n_blk = s // BLK
    assert s % BLK == 0 and n_blk <= LANES
    return pl.pallas_call(
        functools.partial(_moba_kernel, scale=dh ** -0.5, n_blk=n_blk),
        out_shape=jax.ShapeDtypeStruct((bsz, s, heads * dh), BF16),
        grid=(bsz, heads, n_blk),
        in_specs=[
            pl.BlockSpec((1, BLK, dh), lambda b, h, i: (b, i, q_col // dh + h)),
            pl.BlockSpec((1, s, dh), lambda b, h, i: (b, 0, k_col // dh + h)),
            pl.BlockSpec((1, s, dh), lambda b, h, i: (b, 0, v_col // dh + h)),
            pl.BlockSpec((1, dh), lambda b, h, i: (0, 0)),
            pl.BlockSpec((1, dh), lambda b, h, i: (0, 0)),
        ],
        out_specs=pl.BlockSpec((1, BLK, dh), lambda b, h, i: (b, i, h)),
        scratch_shapes=[
            pltpu.VMEM((s, dh), BF16),
            pltpu.VMEM((LANES, dh), BF16),
            pltpu.VMEM((LANES, dh), BF16),
        ],
        compiler_params=_params(("parallel", "parallel", "arbitrary"), "moba"),
        name="moba",
    )(proj, proj, proj, q_norm_g.reshape(1, dh), k_norm_g.reshape(1, dh))


def kernel(x, c, norm_mix_g, norm_ffn_g, ada_w, ada_b, ffn_w_gate, ffn_w_up, ffn_w_down, ab_w_in, ab_w_out, gm_ln_g, gm_ln_b, gm_w_s, gm_b_s, ssd_conv_w, ssd_conv_b, ssd_dt_bias, ssd_a_log, ssd_d, ssd_norm_g, cd_w_in, cd_w_out, moba_q_norm_g, moba_k_norm_g):
    bsz, s, d = x.shape
    depth = ada_w.shape[0]
    mod = _ada_mod(c, ada_w, ada_b)

    a_width = gm_ln_g.shape[1]
    b_width = ssd_norm_g.shape[1]
    conv_ch = ssd_conv_w.shape[2]
    ssd_heads = ssd_dt_bias.shape[1]
    ab_main = 2 * a_width + b_width + conv_ch
    n_att = cd_w_out.shape[1] // ATT_HEAD_DIM
    d_heads = n_att // 4
    c_heads = n_att - d_heads
    c_width = c_heads * ATT_HEAD_DIM
    d_width = d_heads * ATT_HEAD_DIM

    ffn_w = (ffn_w_gate, ffn_w_up, ffn_w_down)
    for layer in range(depth):
        sh_m, sc_m, g_m, sh_f, sc_f, g_f = [mod[layer, :, None, k * d:(k + 1) * d] for k in range(6)]
        if layer % 2 == 0:
            e = layer // 2
            w_in_t = ab_w_in[e].T
            proj, dt_raw = _in_proj(x, norm_mix_g[layer], sc_m, sh_m, w_in_t.astype(BF16), n_cols=ab_main,
                                    n_act_cols=2 * a_width, w_extra=w_in_t[ab_main:], w_transposed=True)
            y_a = _gmlp(proj, gm_ln_g[e], gm_ln_b[e], gm_w_s[e], gm_b_s[e], width=a_width)
            y_b, w_out16, *ffn16 = _ssd(
                proj, dt_raw, ssd_conv_w[e], ssd_conv_b[e], ssd_dt_bias[e], ssd_a_log[e], ssd_d[e], ssd_norm_g[e],
                width=b_width, z_col=2 * a_width, xbc_col=2 * a_width + b_width,
                cast=[(ab_w_out, e)] + [(w, layer) for w in ffn_w])
            x = _out_proj(y_a, y_b, w_out16[0], x, g_m)
        else:
            o = layer // 2
            col_scale = jnp.where(jnp.arange(cd_w_in.shape[2]) < c_width, ATT_HEAD_DIM ** -0.5 * LOG2E, 1.0)
            proj = _in_proj(x, norm_mix_g[layer], sc_m, sh_m, (cd_w_in[o] * col_scale).astype(BF16))
            y_c, w_out16, *ffn16 = _stick_breaking(
                proj, heads=c_heads, q_col=0, k_col=c_width, v_col=2 * c_width,
                cast=[(cd_w_out, o)] + [(w, layer) for w in ffn_w])
            y_d = _moba(proj, moba_q_norm_g[o], moba_k_norm_g[o], heads=d_heads,
                        q_col=3 * c_width, k_col=3 * c_width + d_width, v_col=3 * c_width + 2 * d_width)
            x = _out_proj(y_c, y_d, w_out16[0], x, g_m, tn=1024)
        x = _ffn(x, norm_ffn_g[layer], sc_f, sh_f, g_f, *ffn16, 0)
    return x
```
